```python
import math
import jax, jax.numpy as jnp
from jax import lax
import numpy as np

D_MODEL = 2048
BATCH = 4
SEQ = 2048
DEPTH = 2

CTX_LEN = 256
GRID_W = 64
Q_BLOCK = 128
EPS = 1e-6
ROPE_THETA = 10000.0

DA_HEADS = 6
DA_QK_DIM = 64
DA_V_DIM = 2 * DA_QK_DIM
DA_WIDTH = DA_HEADS * DA_V_DIM
DA_QK_COLS = DA_HEADS * 2 * DA_QK_DIM
DA_SCALE = DA_QK_DIM ** -0.5

GM_GROUPS = 4
GM_CH = 128
GM_CHUNK = 128
GM_WIDTH = GM_GROUPS * GM_CH

MLA_HEADS = 6
MLA_Q_RANK = 512
MLA_KV_RANK = 512
MLA_NOPE = 128
MLA_ROPE = 64
MLA_V = 128
MLA_WIDTH = MLA_HEADS * MLA_V
MLA_SCALE = (MLA_NOPE + MLA_ROPE) ** -0.5

ROT_DIM = 64
MIX_WIDTH = DA_WIDTH + GM_WIDTH + MLA_WIDTH
IN_SIZES = (DA_QK_COLS, DA_QK_COLS, DA_WIDTH, GM_WIDTH, GM_WIDTH, MLA_Q_RANK, MLA_KV_RANK, MLA_ROPE)
IN_COLS = sum(IN_SIZES)
D_FF = 4 * D_MODEL

kernel_name = 'hybrid_diffattn_gmlp_mla_dit'


def rms_norm(x, g):
    xf = x.astype(jnp.float32)
    y = xf * lax.rsqrt(jnp.mean(xf * xf, axis=-1, keepdims=True) + EPS)
    return (y * g.astype(jnp.float32)).astype(x.dtype)


def modulate(h, shift, scale):
    return h * (1.0 + scale) + shift


def axial_rope_tables(rows):
    row = jnp.repeat(jnp.arange(rows, dtype=jnp.float32), GRID_W)
    col = jnp.tile(jnp.arange(GRID_W, dtype=jnp.float32), rows)
    n_f = ROT_DIM // 4
    inv = ROPE_THETA ** (-jnp.arange(n_f, dtype=jnp.float32) / n_f)
    ang = jnp.concatenate([row[:, None] * inv, col[:, None] * inv], axis=-1)
    return jnp.cos(ang), jnp.sin(ang)


def apply_rope(x, cos, sin):
    extra = x.ndim - 3
    cs = cos.reshape(cos.shape[:1] + (1,) * extra + cos.shape[1:]).astype(x.dtype)
    sn = sin.reshape(sin.shape[:1] + (1,) * extra + sin.shape[1:]).astype(x.dtype)
    xp = x.reshape(x.shape[:-1] + (x.shape[-1] // 2, 2))
    x0, x1 = xp[..., 0], xp[..., 1]
    return jnp.stack([x0 * cs - x1 * sn, x0 * sn + x1 * cs], axis=-1).reshape(x.shape)


def split_in(z):
    offs = [int(o) for o in np.cumsum(IN_SIZES)[:-1]]
    return jnp.split(z, offs, axis=-1)


def over_query_blocks(fn, *qs):
    b, n = qs[0].shape[:2]
    nb = n // Q_BLOCK
    blocks = tuple(jnp.moveaxis(q.reshape((b, nb, Q_BLOCK) + q.shape[2:]), 1, 0) for q in qs)
    out = lax.map(lambda blk: fn(*blk), blocks)
    out = jnp.moveaxis(out, 0, 1)
    return out.reshape((b, n) + out.shape[3:])


def diff_attention(q, k, v, lam, scale):
    s = jnp.einsum('bqhcd,bkhcd->bhcqk', q, k).astype(jnp.float32) * scale
    p = jax.nn.softmax(s, axis=-1)
    w = p[:, :, 0] - lam * p[:, :, 1]
    return jnp.einsum('bhqk,bkhd->bqhd', w.astype(v.dtype), v)


def mla_attention(q_nope, q_rope, k_nope, k_rope, v, scale):
    s = (jnp.einsum('bqhd,bkhd->bhqk', q_nope, k_nope)
         + jnp.einsum('bqhr,bkr->bhqk', q_rope, k_rope)).astype(jnp.float32) * scale
    p = jax.nn.softmax(s, axis=-1).astype(v.dtype)
    return jnp.einsum('bhqk,bkhd->bqhd', p, v)


def chunk_spatial_gating(u, v, w_spatial, b_spatial):
    b, n = u.shape[:2]
    vb = v.reshape(b, n // GM_CHUNK, GM_CHUNK, GM_GROUPS, GM_CH)
    mixed = jnp.einsum('gpq,bmqgc->bmpgc', w_spatial, vb) + b_spatial.T[None, None, :, :, None]
    return (u * mixed.reshape(b, n, GM_GROUPS, GM_CH)).reshape(b, n, GM_WIDTH)


def keys_values(z, g_mla_kv, w_mla_ukv, rope):
    b, n, _ = z.shape
    _, k, v, _, _, _, ckv, kr = split_in(z)
    k = k.reshape(b, n, DA_HEADS, 2, DA_QK_DIM)
    v = v.reshape(b, n, DA_HEADS, DA_V_DIM)
    kv = (rms_norm(ckv, g_mla_kv) @ w_mla_ukv).reshape(b, n, MLA_HEADS, MLA_NOPE + MLA_V)
    k_nope, v_mla = kv[..., :MLA_NOPE], kv[..., MLA_NOPE:]
    if rope is not None:
        cos, sin = rope
        k = apply_rope(k, cos, sin)
        kr = apply_rope(kr, cos, sin)
    return k, v, k_nope, kr, v_mla


def queries_and_gating(z, g_gm_v, w_spatial, b_spatial, g_mla_q, w_mla_uq, rope):
    b, n, _ = z.shape
    q, _, _, gu, gv, cq, _, _ = split_in(z)
    q = q.reshape(b, n, DA_HEADS, 2, DA_QK_DIM)
    qm = (rms_norm(cq, g_mla_q) @ w_mla_uq).reshape(b, n, MLA_HEADS, MLA_NOPE + MLA_ROPE)
    q_nope, q_rope = qm[..., :MLA_NOPE], qm[..., MLA_NOPE:]
    if rope is not None:
        cos, sin = rope
        q = apply_rope(q, cos, sin)
        q_rope = apply_rope(q_rope, cos, sin)
    u = jax.nn.gelu(gu, approximate=False).reshape(b, n, GM_GROUPS, GM_CH)
    v = rms_norm(jax.nn.gelu(gv, approximate=False).reshape(b, n, GM_GROUPS, GM_CH), g_gm_v)
    gated = chunk_spatial_gating(u, v, w_spatial, b_spatial)
    return q, q_nope, q_rope, gated


def diff_head_out(o, g_da_sub, lam_init):
    b, n = o.shape[:2]
    return (rms_norm(o, g_da_sub) * (1.0 - lam_init)).reshape(b, n, DA_WIDTH)


def sq_relu_mlp(h, w_fc1, w_fc2):
    return jnp.square(jax.nn.relu(h @ w_fc1)) @ w_fc2


def setup_inputs(seed: int = 0) -> dict:
    key = jax.random.key(seed)
    ks = jax.random.split(key, 25)
    f32 = jnp.float32

    def nrm(k, shape, scale):
        return jax.random.normal(k, shape, f32) * scale

    def gain(k, shape):
        return 1.0 + 0.02 * jax.random.normal(k, shape, f32)

    return {
        'x': nrm(ks[0], (BATCH, SEQ, D_MODEL), 1.0),
        'c': nrm(ks[1], (BATCH, D_MODEL), 1.0),
        'ctx': nrm(ks[2], (BATCH, CTX_LEN, D_MODEL), 1.0),
        'c_ctx': nrm(ks[3], (D_MODEL,), 1.0),
        'w_mod': nrm(ks[4], (DEPTH, D_MODEL, 6 * D_MODEL), 0.5 * D_MODEL ** -0.5),
        'b_mod': nrm(ks[5], (DEPTH, 6 * D_MODEL), 0.01),
        'g_norm_mix': gain(ks[6], (DEPTH, D_MODEL)),
        'g_norm_mlp': gain(ks[7], (DEPTH, D_MODEL)),
        'w_in': nrm(ks[8], (DEPTH, D_MODEL, IN_COLS), D_MODEL ** -0.5),
        'lam_q1': nrm(ks[9], (DEPTH, DA_QK_DIM), 0.1),
        'lam_k1': nrm(ks[10], (DEPTH, DA_QK_DIM), 0.1),
        'lam_q2': nrm(ks[11], (DEPTH, DA_QK_DIM), 0.1),
        'lam_k2': nrm(ks[12], (DEPTH, DA_QK_DIM), 0.1),
        'g_da_sub': gain(ks[13], (DEPTH, DA_V_DIM)),
        'g_gm_v': gain(ks[14], (DEPTH, GM_GROUPS, GM_CH)),
        'w_spatial': nrm(ks[15], (DEPTH, GM_GROUPS, GM_CHUNK, GM_CHUNK), GM_CHUNK ** -0.5),
        'b_spatial': gain(ks[16], (DEPTH, GM_GROUPS, GM_CHUNK)),
        'g_mla_q': gain(ks[17], (DEPTH, MLA_Q_RANK)),
        'w_mla_uq': nrm(ks[18], (DEPTH, MLA_Q_RANK, MLA_HEADS * (MLA_NOPE + MLA_ROPE)), MLA_Q_RANK ** -0.5),
        'g_mla_kv': gain(ks[19], (DEPTH, MLA_KV_RANK)),
        'w_mla_ukv': nrm(ks[20], (DEPTH, MLA_KV_RANK, MLA_HEADS * (MLA_NOPE + MLA_V)), MLA_KV_RANK ** -0.5),
        'w_out': nrm(ks[21], (DEPTH, MIX_WIDTH, D_MODEL), MIX_WIDTH ** -0.5),
        'w_fc1': nrm(ks[22], (DEPTH, D_MODEL, D_FF), D_MODEL ** -0.5),
        'w_fc2': nrm(ks[23], (DEPTH, D_FF, D_MODEL), D_FF ** -0.5),
        'g_final': gain(ks[24], (D_MODEL,)),
    }


def reference(x, c, ctx, c_ctx, w_mod, b_mod, g_norm_mix, g_norm_mlp, w_in,
              lam_q1, lam_k1, lam_q2, lam_k2, g_da_sub, g_gm_v, w_spatial, b_spatial,
              g_mla_q, w_mla_uq, g_mla_kv, w_mla_ukv, w_out, w_fc1, w_fc2, g_final):
    n = x.shape[1]
    rows = n // GRID_W
    rope = axial_rope_tables(rows)
    xc = ctx
    sc_lat = jax.nn.silu(c)
    sc_ctx = jax.nn.silu(c_ctx)

    for l in range(DEPTH):
        update_ctx = l < DEPTH - 1
        lam_init = 0.8 - 0.6 * math.exp(-0.3 * l)
        f32 = jnp.float32
        lam = (jnp.exp(jnp.sum(lam_q1[l].astype(f32) * lam_k1[l].astype(f32)))
               - jnp.exp(jnp.sum(lam_q2[l].astype(f32) * lam_k2[l].astype(f32))) + lam_init)

        mod = sc_lat @ w_mod[l] + b_mod[l]
        mod_c = sc_ctx @ w_mod[l] + b_mod[l]
        sh1, sc1, gt1, sh2, sc2, gt2 = jnp.split(mod[:, None, :], 6, axis=-1)
        csh1, csc1, cgt1, csh2, csc2, cgt2 = jnp.split(mod_c, 6, axis=-1)

        h = modulate(rms_norm(x, g_norm_mix[l]), sh1, sc1)
        hc = modulate(rms_norm(xc, g_norm_mix[l]), csh1, csc1)
        z = h @ w_in[l]
        zc = hc @ w_in[l]

        k, v, kn, kr, vm = keys_values(z, g_mla_kv[l], w_mla_ukv[l], rope)
        kc, vc, knc, krc, vmc = keys_values(zc, g_mla_kv[l], w_mla_ukv[l], None)
        q, qn, qr, gated = queries_and_gating(z, g_gm_v[l], w_spatial[l], b_spatial[l],
                                              g_mla_q[l], w_mla_uq[l], rope)

        k_all = jnp.concatenate([kc, k], axis=1)
        v_all = jnp.concatenate([vc, v], axis=1)
        kn_all = jnp.concatenate([knc, kn], axis=1)
        kr_all = jnp.concatenate([krc, kr], axis=1)
        vm_all = jnp.concatenate([vmc, vm], axis=1)
        o_da = over_query_blocks(lambda qb: diff_attention(qb, k_all, v_all, lam, DA_SCALE), q)
        o_mla = over_query_blocks(
            lambda qnb, qrb: mla_attention(qnb, qrb, kn_all, kr_all, vm_all, MLA_SCALE), qn, qr)
        heads = jnp.concatenate([diff_head_out(o_da, g_da_sub[l], lam_init), gated,
                                 o_mla.reshape(o_mla.shape[0], n, MLA_WIDTH)], axis=-1)
        x = x + gt1 * (heads @ w_out[l])

        h2 = modulate(rms_norm(x, g_norm_mlp[l]), sh2, sc2)
        x = x + gt2 * sq_relu_mlp(h2, w_fc1[l], w_fc2[l])

        if update_ctx:
            qc, qnc, qrc, gated_c = queries_and_gating(zc, g_gm_v[l], w_spatial[l], b_spatial[l],
                                                       g_mla_q[l], w_mla_uq[l], None)
            o_da_c = diff_attention(qc, kc, vc, lam, DA_SCALE)
            o_mla_c = mla_attention(qnc, qrc, knc, krc, vmc, MLA_SCALE)
            heads_c = jnp.concatenate([diff_head_out(o_da_c, g_da_sub[l], lam_init), gated_c,
                                       o_mla_c.reshape(o_mla_c.shape[0], CTX_LEN, MLA_WIDTH)], axis=-1)
            xc = xc + cgt1 * (heads_c @ w_out[l])
            h2c = modulate(rms_norm(xc, g_norm_mlp[l]), csh2, csc2)
            xc = xc + cgt2 * sq_relu_mlp(h2c, w_fc1[l], w_fc2[l])

    return rms_norm(x, g_final)
```

```python
import functools
import math

import numpy as np
import jax
import jax.numpy as jnp
from jax import lax
from jax.experimental import pallas as pl
from jax.experimental.pallas import tpu as pltpu

D_MODEL = 2048
BATCH = 4
SEQ = 2048
DEPTH = 2
CTX_LEN = 256
GRID_W = 64
EPS = 1e-6
ROPE_THETA = 10000.0

DA_HEADS = 6
DA_QK_DIM = 64
DA_V_DIM = 128
DA_WIDTH = DA_HEADS * DA_V_DIM
DA_QK_COLS = DA_HEADS * 2 * DA_QK_DIM
DA_SCALE = DA_QK_DIM ** -0.5

GM_GROUPS = 4
GM_CH = 128
GM_CHUNK = 128
GM_WIDTH = GM_GROUPS * GM_CH

MLA_HEADS = 6
MLA_Q_RANK = 512
MLA_KV_RANK = 512
MLA_NOPE = 128
MLA_ROPE = 64
MLA_V = 128
MLA_WIDTH = MLA_HEADS * MLA_V
MLA_SCALE = (MLA_NOPE + MLA_ROPE) ** -0.5

ROT_DIM = 64
D_FF = 4 * D_MODEL
MOD_ROWS = 8
CTX_MOD_ROW = BATCH

LANE = 128
Z_Q, Z_K, Z_V, Z_GU, Z_GV, Z_CQ, Z_CKV, Z_KR = 0, 768, 1536, 2304, 2816, 3328, 3840, 4352
Z_COLS = Z_KR + LANE
MLA_QH = 2 * LANE

VMEM_LIMIT = 56 * 1024 * 1024

BF16 = jnp.bfloat16
F32 = jnp.float32


def _cparams(sem):
    return pltpu.CompilerParams(dimension_semantics=sem, vmem_limit_bytes=VMEM_LIMIT)


def _deinterleave(w):
    lead = w.shape[:-1]
    return w.reshape(lead + (-1, ROT_DIM // 2, 2)).swapaxes(-1, -2).reshape(w.shape)


def _w_in_layout(w):
    k = w.shape[0]
    return jnp.concatenate([
        _deinterleave(w[:, :DA_QK_COLS]) * DA_SCALE,
        _deinterleave(w[:, DA_QK_COLS:2 * DA_QK_COLS]),
        w[:, 2 * DA_QK_COLS:-MLA_ROPE],
        _deinterleave(w[:, -MLA_ROPE:]),
        jnp.zeros((k, LANE - MLA_ROPE), w.dtype)], axis=1).astype(BF16)


def _w_uq_layout(w):
    k = w.shape[0]
    w = w.reshape(k, MLA_HEADS, MLA_NOPE + MLA_ROPE)
    return jnp.concatenate([
        w[:, :, :MLA_NOPE], _deinterleave(w[:, :, MLA_NOPE:]),
        jnp.zeros((k, MLA_HEADS, MLA_QH - MLA_NOPE - MLA_ROPE), w.dtype)], axis=2
    ).reshape(k, MLA_HEADS * MLA_QH).astype(BF16)


def _w_ukv_layout(w):
    k = w.shape[0]
    w = w.reshape(k, MLA_HEADS, MLA_NOPE + MLA_V)
    return jnp.concatenate([w[:, :, :MLA_NOPE].reshape(k, -1), w[:, :, MLA_NOPE:].reshape(k, -1)],
                           axis=1).astype(BF16)


def _rope_tables():
    rows = SEQ // GRID_W
    row = jnp.repeat(jnp.arange(rows, dtype=F32), GRID_W)
    col = jnp.tile(jnp.arange(GRID_W, dtype=F32), rows)
    n_f = ROT_DIM // 4
    inv = ROPE_THETA ** (-jnp.arange(n_f, dtype=F32) / n_f)
    ang = jnp.concatenate([row[:, None] * inv, col[:, None] * inv], axis=-1)
    cos, sin = jnp.cos(ang), jnp.sin(ang)
    cos_t = jnp.concatenate([cos, cos, cos, cos], axis=-1)
    sin_t = jnp.concatenate([-sin, sin, -sin, sin], axis=-1)
    return cos_t, sin_t


def _rms(x, g):
    return x * lax.rsqrt(jnp.mean(x * x, axis=-1, keepdims=True) + EPS) * g


def _gelu(x):
    return 0.5 * x * (1.0 + lax.erf(x * math.sqrt(0.5)))


def _rope128(x, cos, sin):
    lane = lax.broadcasted_iota(jnp.int32, x.shape, 1)
    partner = jnp.where((lane & 32) == 0, pltpu.roll(x, LANE - 32, 1), pltpu.roll(x, 32, 1))
    return x * cos + partner * sin


def _mod_kernel(c_ref, w_ref, b_ref, o_ref):
    c = c_ref[...]
    s = (c / (1.0 + jnp.exp(-c))).astype(BF16)
    o_ref[0] = jnp.dot(s, w_ref[0].astype(BF16), preferred_element_type=F32) + b_ref[0]


def _inproj_kernel(x_ref, g_ref, sh_ref, sc_ref, w_ref, z_ref, h_ref):
    @pl.when(pl.program_id(1) == 0)
    def _():
        y = _rms(x_ref[...], g_ref[...])
        h_ref[...] = (y * (1.0 + sc_ref[0]) + sh_ref[0]).astype(BF16)

    z_ref[...] = jnp.dot(h_ref[...], w_ref[...], preferred_element_type=F32)


def _zpost_kernel(z_ref, cos_ref, sin_ref, ggm_ref, wsp_ref, bsp_ref, gq_ref, wuq_ref, gkv_ref, wukv_ref,
                  q_ref, k_ref, v_ref, gated_ref, qm_ref, kn_ref, vm_ref, kr_ref):
    tm = z_ref.shape[0]
    cos, sin = cos_ref[...], sin_ref[...]
    for j in range(DA_HEADS):
        sl = slice(LANE * j, LANE * (j + 1))
        q_ref[:, sl] = _rope128(z_ref[:, Z_Q + LANE * j:Z_Q + LANE * (j + 1)], cos, sin).astype(BF16)
        k_ref[:, sl] = _rope128(z_ref[:, Z_K + LANE * j:Z_K + LANE * (j + 1)], cos, sin).astype(BF16)
    v_ref[...] = z_ref[:, Z_V:Z_GU].astype(BF16)
    kr_ref[...] = _rope128(z_ref[:, Z_KR:Z_COLS], cos, sin).astype(BF16)

    for g in range(GM_GROUPS):
        sl = slice(GM_CH * g, GM_CH * (g + 1))
        u = _gelu(z_ref[:, Z_GU + GM_CH * g:Z_GU + GM_CH * (g + 1)])
        v = _rms(_gelu(z_ref[:, Z_GV + GM_CH * g:Z_GV + GM_CH * (g + 1)]), ggm_ref[:, sl]).astype(BF16)
        for c in range(tm // GM_CHUNK):
            rows = slice(GM_CHUNK * c, GM_CHUNK * (c + 1))
            mixed = jnp.dot(wsp_ref[g], v[rows], preferred_element_type=F32) + bsp_ref[g]
            gated_ref[rows, sl] = (u[rows] * mixed).astype(BF16)

    cq = _rms(z_ref[:, Z_CQ:Z_CKV], gq_ref[...]).astype(BF16)
    qm = jnp.dot(cq, wuq_ref[...], preferred_element_type=F32)
    for h in range(MLA_HEADS):
        lo = MLA_QH * h
        qm_ref[:, lo:lo + LANE] = (qm[:, lo:lo + LANE] * MLA_SCALE).astype(BF16)
        qm_ref[:, lo + LANE:lo + MLA_QH] = (_rope128(qm[:, lo + LANE:lo + MLA_QH], cos, sin) * MLA_SCALE).astype(BF16)

    ckv = _rms(z_ref[:, Z_CKV:Z_KR], gkv_ref[...]).astype(BF16)
    kv = jnp.dot(ckv, wukv_ref[...], preferred_element_type=F32)
    kn_ref[...] = kv[:, :MLA_HEADS * MLA_NOPE].astype(BF16)
    vm_ref[...] = kv[:, MLA_HEADS * MLA_NOPE:].astype(BF16)


def _nt_dot(a, b):
    return lax.dot_general(a, b, (((1,), (1,)), ((), ())), preferred_element_type=F32)


def _softmax_pieces(scores):
    m = functools.reduce(jnp.maximum, [jnp.max(s, axis=-1, keepdims=True) for s in scores])
    es = [jnp.exp(s - m) for s in scores]
    total = functools.reduce(jnp.add, [jnp.sum(e, axis=-1, keepdims=True) for e in es])
    return es, 1.0 / total


def _da_attn_kernel(lam_ref, g_ref, q_ref, *refs, n_pieces, lam_init):
    k_refs, v_refs, o_ref = refs[:n_pieces], refs[n_pieces:2 * n_pieces], refs[2 * n_pieces]
    lv = lam_ref[...]
    lam = (jnp.exp(jnp.sum(lv[0:1] * lv[1:2], axis=-1, keepdims=True))
           - jnp.exp(jnp.sum(lv[2:3] * lv[3:4], axis=-1, keepdims=True)) + lam_init)
    q = q_ref[...]
    lane = lax.broadcasted_iota(jnp.int32, q.shape, 1)
    zero = jnp.zeros_like(q)
    q0 = jnp.where(lane < DA_QK_DIM, q, zero)
    q1 = jnp.where(lane >= DA_QK_DIM, q, zero)
    e0, r0 = _softmax_pieces([_nt_dot(q0, k[...]) for k in k_refs])
    e1, r1 = _softmax_pieces([_nt_dot(q1, k[...]) for k in k_refs])
    r1 = lam * r1
    o = functools.reduce(jnp.add, [
        jnp.dot((a * r0 - b * r1).astype(BF16), v[...], preferred_element_type=F32)
        for a, b, v in zip(e0, e1, v_refs)])
    o_ref[...] = (_rms(o, g_ref[...]) * (1.0 - lam_init)).astype(BF16)


def _mla_attn_kernel(q_ref, *refs, n_pieces):
    kn_refs, kr_refs = refs[:n_pieces], refs[n_pieces:2 * n_pieces]
    vm_refs, o_ref = refs[2 * n_pieces:3 * n_pieces], refs[3 * n_pieces]
    qn, qr = q_ref[:, :LANE], q_ref[:, LANE:]
    es, r = _softmax_pieces([_nt_dot(qn, kn[...]) + _nt_dot(qr, kr[...]) for kn, kr in zip(kn_refs, kr_refs)])
    o = functools.reduce(jnp.add, [
        jnp.dot((e * r).astype(BF16), v[...], preferred_element_type=F32) for e, v in zip(es, vm_refs)])
    o_ref[...] = o.astype(BF16)


def _outproj_kernel(x_ref, gt_ref, a_ref, b_ref, c_ref, wa_ref, wb_ref, wc_ref, o_ref):
    acc = (jnp.dot(a_ref[...], wa_ref[...], preferred_element_type=F32)
           + jnp.dot(b_ref[...], wb_ref[...], preferred_element_type=F32)
           + jnp.dot(c_ref[...], wc_ref[...], preferred_element_type=F32))
    o_ref[...] = x_ref[...] + gt_ref[0] * acc


def _mlp_kernel(x_ref, g_ref, sh_ref, sc_ref, gt_ref, gf_ref, w1_ref, w2_ref, o_ref, h_ref, acc_ref, *, final_norm):
    f = pl.program_id(1)

    @pl.when(f == 0)
    def _():
        y = _rms(x_ref[...], g_ref[...])
        h_ref[...] = (y * (1.0 + sc_ref[0]) + sh_ref[0]).astype(BF16)
        acc_ref[...] = jnp.zeros_like(acc_ref)

    a = jnp.maximum(jnp.dot(h_ref[...], w1_ref[...], preferred_element_type=F32), 0.0)
    acc_ref[...] += jnp.dot((a * a).astype(BF16), w2_ref[...], preferred_element_type=F32)

    @pl.when(f == pl.num_programs(1) - 1)
    def _():
        out = x_ref[...] + gt_ref[0] * acc_ref[...]
        if final_norm:
            out = _rms(out, gf_ref[...])
        o_ref[...] = out


def _mod_spec(row_of, chunk):
    return pl.BlockSpec((1, 1, D_MODEL), lambda i, *_: (row_of(i), 0, chunk))


def _mod_call(cvec, w_mod, b_mod):
    tn = 1024
    return pl.pallas_call(
        _mod_kernel,
        grid=(DEPTH, 6 * D_MODEL // tn),
        in_specs=[pl.BlockSpec((MOD_ROWS, D_MODEL), lambda l, n: (0, 0)),
                  pl.BlockSpec((1, D_MODEL, tn), lambda l, n: (l, 0, n)),
                  pl.BlockSpec((1, 1, tn), lambda l, n: (l, 0, n))],
        out_specs=pl.BlockSpec((1, MOD_ROWS, tn), lambda l, n: (l, 0, n)),
        out_shape=jax.ShapeDtypeStruct((DEPTH, MOD_ROWS, 6 * D_MODEL), F32),
        compiler_params=_cparams(("arbitrary", "arbitrary")),
        name="mod",
    )(cvec, w_mod, b_mod.reshape(DEPTH, 1, 6 * D_MODEL))


def _inproj_call(x, g, mod, row_of, w):
    m = x.shape[0]
    tm, tn = 512, 640
    return pl.pallas_call(
        _inproj_kernel,
        grid=(m // tm, Z_COLS // tn),
        in_specs=[pl.BlockSpec((tm, D_MODEL), lambda i, n: (i, 0)),
                  pl.BlockSpec((1, D_MODEL), lambda i, n: (0, 0)),
                  _mod_spec(row_of, 0), _mod_spec(row_of, 1),
                  pl.BlockSpec((D_MODEL, tn), lambda i, n: (0, n))],
        out_specs=pl.BlockSpec((tm, tn), lambda i, n: (i, n)),
        out_shape=jax.ShapeDtypeStruct((m, Z_COLS), F32),
        scratch_shapes=[pltpu.VMEM((tm, D_MODEL), BF16)],
        compiler_params=_cparams(("parallel", "arbitrary")),
        name="inproj",
    )(x, g, mod, mod, w)


def _zpost_call(z, cos_t, sin_t, ggm, wsp, bsp, gq, wuq, gkv, wukv):
    m = z.shape[0]
    tm = 256
    pos_tiles = cos_t.shape[0] // tm
    row = lambda i: (i, 0)
    const2 = lambda i: (0, 0)
    const3 = lambda i: (0, 0, 0)
    widths = (DA_QK_COLS, DA_QK_COLS, DA_WIDTH, GM_WIDTH, MLA_HEADS * MLA_QH,
              MLA_HEADS * MLA_NOPE, MLA_WIDTH, LANE)
    return pl.pallas_call(
        _zpost_kernel,
        grid=(m // tm,),
        in_specs=[pl.BlockSpec((tm, Z_COLS), row),
                  pl.BlockSpec((tm, LANE), lambda i: (i % pos_tiles, 0)),
                  pl.BlockSpec((tm, LANE), lambda i: (i % pos_tiles, 0)),
                  pl.BlockSpec((1, GM_WIDTH), const2),
                  pl.BlockSpec((GM_GROUPS, GM_CHUNK, GM_CHUNK), const3),
                  pl.BlockSpec((GM_GROUPS, GM_CHUNK, GM_CH), const3),
                  pl.BlockSpec((1, MLA_Q_RANK), const2),
                  pl.BlockSpec((MLA_Q_RANK, MLA_HEADS * MLA_QH), const2),
                  pl.BlockSpec((1, MLA_KV_RANK), const2),
                  pl.BlockSpec((MLA_KV_RANK, MLA_HEADS * (MLA_NOPE + MLA_V)), const2)],
        out_specs=[pl.BlockSpec((tm, w), row) for w in widths],
        out_shape=[jax.ShapeDtypeStruct((m, w), BF16) for w in widths],
        compiler_params=_cparams(("parallel",)),
        name="zpost",
    )(z, cos_t, sin_t, ggm, wsp, bsp, gq, wuq, gkv, wukv)


def _kv_specs(arrays, width):
    return [pl.BlockSpec((a.shape[0] // BATCH, width), lambda b, h, t: (b, h)) for a in arrays]


def _da_attn_call(lamv, g, q, ks, vs, lam_init):
    nq = q.shape[0] // BATCH
    tq = 256
    nqt = nq // tq
    qspec = pl.BlockSpec((tq, LANE), lambda b, h, t: (b * nqt + t, h))
    return pl.pallas_call(
        functools.partial(_da_attn_kernel, n_pieces=len(ks), lam_init=lam_init),
        grid=(BATCH, DA_HEADS, nqt),
        in_specs=[pl.BlockSpec(lamv.shape, lambda b, h, t: (0, 0)),
                  pl.BlockSpec((1, DA_V_DIM), lambda b, h, t: (0, 0)),
                  qspec] + _kv_specs(ks, LANE) + _kv_specs(vs, LANE),
        out_specs=qspec,
        out_shape=jax.ShapeDtypeStruct((q.shape[0], DA_WIDTH), BF16),
        compiler_params=_cparams(("parallel", "parallel", "arbitrary")),
        name="da_attn",
    )(lamv, g, q, *ks, *vs)


def _mla_attn_call(qm, kns, krs, vms):
    nq = qm.shape[0] // BATCH
    tq = 256
    nqt = nq // tq
    kr_specs = [pl.BlockSpec((a.shape[0] // BATCH, LANE), lambda b, h, t: (b, 0)) for a in krs]
    return pl.pallas_call(
        functools.partial(_mla_attn_kernel, n_pieces=len(kns)),
        grid=(BATCH, MLA_HEADS, nqt),
        in_specs=[pl.BlockSpec((tq, MLA_QH), lambda b, h, t: (b * nqt + t, h))]
        + _kv_specs(kns, LANE) + kr_specs + _kv_specs(vms, LANE),
        out_specs=pl.BlockSpec((tq, MLA_V), lambda b, h, t: (b * nqt + t, h)),
        out_shape=jax.ShapeDtypeStruct((qm.shape[0], MLA_WIDTH), BF16),
        compiler_params=_cparams(("parallel", "parallel", "arbitrary")),
        name="mla_attn",
    )(qm, *kns, *krs, *vms)


def _outproj_call(x, mod, row_of, o_da, gated, o_mla, wa, wb, wc):
    m = x.shape[0]
    tm = 256
    row = lambda i: (i, 0)
    const = lambda i: (0, 0)
    return pl.pallas_call(
        _outproj_kernel,
        grid=(m // tm,),
        in_specs=[pl.BlockSpec((tm, D_MODEL), row), _mod_spec(row_of, 2),
                  pl.BlockSpec((tm, DA_WIDTH), row), pl.BlockSpec((tm, GM_WIDTH), row),
                  pl.BlockSpec((tm, MLA_WIDTH), row),
                  pl.BlockSpec((DA_WIDTH, D_MODEL), const), pl.BlockSpec((GM_WIDTH, D_MODEL), const),
                  pl.BlockSpec((MLA_WIDTH, D_MODEL), const)],
        out_specs=pl.BlockSpec((tm, D_MODEL), row),
        out_shape=jax.ShapeDtypeStruct((m, D_MODEL), F32),
        compiler_params=_cparams(("parallel",)),
        name="outproj",
    )(x, mod, o_da, gated, o_mla, wa, wb, wc)


def _mlp_call(x, g, mod, row_of, g_final, w1, w2, final_norm):
    m = x.shape[0]
    tm, tf = 512, 512
    row = lambda i, f: (i, 0)
    return pl.pallas_call(
        functools.partial(_mlp_kernel, final_norm=final_norm),
        grid=(m // tm, D_FF // tf),
        in_specs=[pl.BlockSpec((tm, D_MODEL), row),
                  pl.BlockSpec((1, D_MODEL), lambda i, f: (0, 0)),
                  _mod_spec(row_of, 3), _mod_spec(row_of, 4), _mod_spec(row_of, 5),
                  pl.BlockSpec((1, D_MODEL), lambda i, f: (0, 0)),
                  pl.BlockSpec((D_MODEL, tf), lambda i, f: (0, f)),
                  pl.BlockSpec((tf, D_MODEL), lambda i, f: (f, 0))],
        out_specs=pl.BlockSpec((tm, D_MODEL), row),
        out_shape=jax.ShapeDtypeStruct((m, D_MODEL), F32),
        scratch_shapes=[pltpu.VMEM((tm, D_MODEL), BF16), pltpu.VMEM((tm, D_MODEL), F32)],
        compiler_params=_cparams(("parallel", "arbitrary")),
        name="mlp",
    )(x, g, mod, mod, mod, g_final, w1, w2)


def kernel(x, c, ctx, c_ctx, w_mod, b_mod, g_norm_mix, g_norm_mlp, w_in, lam_q1, lam_k1, lam_q2, lam_k2,
           g_da_sub, g_gm_v, w_spatial, b_spatial, g_mla_q, w_mla_uq, g_mla_kv, w_mla_ukv, w_out, w_fc1,
           w_fc2, g_final):
    cos_t, sin_t = _rope_tables()
    cos_c = jnp.ones((CTX_LEN, LANE), F32)
    sin_c = jnp.zeros((CTX_LEN, LANE), F32)

    xl = x.reshape(BATCH * SEQ, D_MODEL)
    xc = ctx.reshape(BATCH * CTX_LEN, D_MODEL)
    cvec = jnp.zeros((MOD_ROWS, D_MODEL), F32).at[:BATCH].set(c).at[CTX_MOD_ROW].set(c_ctx)
    mod_all = _mod_call(cvec, w_mod, b_mod)

    lat_tiles = SEQ // 512
    lat_row_512 = lambda i: i // lat_tiles
    lat_row_256 = lambda i: i // (SEQ // 256)
    ctx_row = lambda i: CTX_MOD_ROW
    g_final2 = g_final.reshape(1, D_MODEL)

    for l in range(DEPTH):
        last = l == DEPTH - 1
        lam_init = 0.8 - 0.6 * math.exp(-0.3 * l)
        mod = mod_all[l].reshape(MOD_ROWS, 1, 6 * D_MODEL)
        w_in_p = _w_in_layout(w_in[l])
        wuq_p = _w_uq_layout(w_mla_uq[l])
        wukv_p = _w_ukv_layout(w_mla_ukv[l])
        w_out_b = w_out[l].astype(BF16)
        wa, wb, wc = w_out_b[:DA_WIDTH], w_out_b[DA_WIDTH:DA_WIDTH + GM_WIDTH], w_out_b[DA_WIDTH + GM_WIDTH:]
        w1, w2 = w_fc1[l].astype(BF16), w_fc2[l].astype(BF16)
        g_mix, g_mlp = g_norm_mix[l].reshape(1, D_MODEL), g_norm_mlp[l].reshape(1, D_MODEL)
        lamv = jnp.stack([lam_q1[l], lam_k1[l], lam_q2[l], lam_k2[l]])
        g_sub = g_da_sub[l].reshape(1, DA_V_DIM)
        post_w = (g_gm_v[l].reshape(1, GM_WIDTH), w_spatial[l].astype(BF16),
                  jnp.broadcast_to(b_spatial[l][:, :, None], (GM_GROUPS, GM_CHUNK, GM_CH)),
                  g_mla_q[l].reshape(1, MLA_Q_RANK), wuq_p, g_mla_kv[l].reshape(1, MLA_KV_RANK), wukv_p)

        z_l = _inproj_call(xl, g_mix, mod, lat_row_512, w_in_p)
        z_c = _inproj_call(xc, g_mix, mod, ctx_row, w_in_p)
        q, k, v, gated, qm, kn, vm, kr = _zpost_call(z_l, cos_t, sin_t, *post_w)
        qc, kc, vc, gated_c, qmc, knc, vmc, krc = _zpost_call(z_c, cos_c, sin_c, *post_w)

        o_da = _da_attn_call(lamv, g_sub, q, [k, kc], [v, vc], lam_init)
        o_mla = _mla_attn_call(qm, [kn, knc], [kr, krc], [vm, vmc])
        xl = _outproj_call(xl, mod, lat_row_256, o_da, gated, o_mla, wa, wb, wc)
        xl = _mlp_call(xl, g_mlp, mod, lat_row_512, g_final2, w1, w2, final_norm=last)

        if not last:
            o_da_c = _da_attn_call(lamv, g_sub, qc, [kc], [vc], lam_init)
            o_mla_c = _mla_attn_call(qmc, [knc], [krc], [vmc])
            xc = _outproj_call(xc, mod, ctx_row, o_da_c, gated_c, o_mla_c, wa, wb, wc)
            xc = _mlp_call(xc, g_mlp, mod, ctx_row, g_final2, w1, w2, final_norm=False)

    return xl.reshape(BATCH, SEQ, D_MODEL)
```

```python
import functools
import math

import numpy as np
import jax
import jax.numpy as jnp
from jax import lax
from jax.experimental import pallas as pl
from jax.experimental.pallas import tpu as pltpu

D_MODEL = 2048
BATCH = 4
SEQ = 2048
DEPTH = 2
CTX_LEN = 256
GRID_W = 64
EPS = 1e-6
ROPE_THETA = 10000.0

DA_HEADS = 6
DA_QK_DIM = 64
DA_V_DIM = 128
DA_WIDTH = DA_HEADS * DA_V_DIM
DA_QK_COLS = DA_HEADS * 2 * DA_QK_DIM
DA_SCALE = DA_QK_DIM ** -0.5

GM_GROUPS = 4
GM_CH = 128
GM_CHUNK = 128
GM_WIDTH = GM_GROUPS * GM_CH

MLA_HEADS = 6
MLA_Q_RANK = 512
MLA_KV_RANK = 512
MLA_NOPE = 128
MLA_ROPE = 64
MLA_V = 128
MLA_WIDTH = MLA_HEADS * MLA_V
MLA_SCALE = (MLA_NOPE + MLA_ROPE) ** -0.5
LOG2E = math.log2(math.e)

ROT_DIM = 64
D_FF = 4 * D_MODEL
MOD_ROWS = 8
CTX_MOD_ROW = BATCH

LANE = 128
Z_Q, Z_K, Z_V, Z_GU, Z_GV, Z_CQ, Z_CKV, Z_KR = 0, 768, 1536, 2304, 2816, 3328, 3840, 4352
Z_COLS = Z_KR + LANE
MLA_QH = 2 * LANE

VMEM_LIMIT = 56 * 1024 * 1024
ATTN_TQ = 1024
ATTN_SUB = 128

BF16 = jnp.bfloat16
F32 = jnp.float32


def _cparams(sem):
    return pltpu.CompilerParams(dimension_semantics=sem, vmem_limit_bytes=VMEM_LIMIT)


def _deinterleave(w):
    lead = w.shape[:-1]
    return w.reshape(lead + (-1, ROT_DIM // 2, 2)).swapaxes(-1, -2).reshape(w.shape)


def _w_in_layout(w):
    k = w.shape[0]
    return jnp.concatenate([
        _deinterleave(w[:, :DA_QK_COLS]) * DA_SCALE,
        _deinterleave(w[:, DA_QK_COLS:2 * DA_QK_COLS]),
        w[:, 2 * DA_QK_COLS:-MLA_ROPE],
        _deinterleave(w[:, -MLA_ROPE:]),
        jnp.zeros((k, LANE - MLA_ROPE), w.dtype)], axis=1).astype(BF16)


def _w_uq_layout(w):
    k = w.shape[0]
    w = w.reshape(k, MLA_HEADS, MLA_NOPE + MLA_ROPE)
    return jnp.concatenate([
        w[:, :, :MLA_NOPE], _deinterleave(w[:, :, MLA_NOPE:]),
        jnp.zeros((k, MLA_HEADS, MLA_QH - MLA_NOPE - MLA_ROPE), w.dtype)], axis=2
    ).reshape(k, MLA_HEADS * MLA_QH).astype(BF16)


def _w_ukv_layout(w):
    k = w.shape[0]
    w = w.reshape(k, MLA_HEADS, MLA_NOPE + MLA_V)
    return jnp.concatenate([w[:, :, :MLA_NOPE].reshape(k, -1), w[:, :, MLA_NOPE:].reshape(k, -1)],
                           axis=1).astype(BF16)


def _rope_tables():
    rows = SEQ // GRID_W
    row = jnp.repeat(jnp.arange(rows, dtype=F32), GRID_W)
    col = jnp.tile(jnp.arange(GRID_W, dtype=F32), rows)
    n_f = ROT_DIM // 4
    inv = ROPE_THETA ** (-jnp.arange(n_f, dtype=F32) / n_f)
    ang = jnp.concatenate([row[:, None] * inv, col[:, None] * inv], axis=-1)
    cos, sin = jnp.cos(ang), jnp.sin(ang)
    cos_t = jnp.concatenate([cos, cos, cos, cos], axis=-1)
    sin_t = jnp.concatenate([-sin, sin, -sin, sin], axis=-1)
    return cos_t, sin_t


def _rms(x, g):
    return x * lax.rsqrt(jnp.mean(x * x, axis=-1, keepdims=True) + EPS) * g


def _gelu(x):
    return 0.5 * x * (1.0 + lax.erf(x * math.sqrt(0.5)))


def _rope128(x, cos, sin):
    lane = lax.broadcasted_iota(jnp.int32, x.shape, 1)
    partner = jnp.where((lane & 32) == 0, pltpu.roll(x, LANE - 32, 1), pltpu.roll(x, 32, 1))
    return x * cos + partner * sin


def _mod_kernel(c_ref, w_ref, b_ref, o_ref):
    c = c_ref[...]
    s = (c / (1.0 + jnp.exp(-c))).astype(BF16)
    o_ref[0] = jnp.dot(s, w_ref[0].astype(BF16), preferred_element_type=F32) + b_ref[0]


def _inproj_kernel(x_ref, g_ref, sh_ref, sc_ref, w_ref, z_ref, h_ref):
    @pl.when(pl.program_id(1) == 0)
    def _():
        y = _rms(x_ref[...], g_ref[...])
        h_ref[...] = (y * (1.0 + sc_ref[0]) + sh_ref[0]).astype(BF16)

    z_ref[...] = jnp.dot(h_ref[...], w_ref[...], preferred_element_type=F32)


def _zpost_kernel(z_ref, cos_ref, sin_ref, ggm_ref, wsp_ref, bsp_ref, gq_ref, wuq_ref, gkv_ref, wukv_ref,
                  q_ref, k_ref, v_ref, gated_ref, qm_ref, kn_ref, vm_ref, kr_ref):
    tm = z_ref.shape[0]
    cos, sin = cos_ref[...], sin_ref[...]
    for j in range(DA_HEADS):
        sl = slice(LANE * j, LANE * (j + 1))
        q_ref[:, sl] = (_rope128(z_ref[:, Z_Q + LANE * j:Z_Q + LANE * (j + 1)], cos, sin) * LOG2E).astype(BF16)
        k_ref[:, sl] = _rope128(z_ref[:, Z_K + LANE * j:Z_K + LANE * (j + 1)], cos, sin).astype(BF16)
    v_ref[...] = z_ref[:, Z_V:Z_GU].astype(BF16)
    kr_ref[...] = _rope128(z_ref[:, Z_KR:Z_COLS], cos, sin).astype(BF16)

    for g in range(GM_GROUPS):
        sl = slice(GM_CH * g, GM_CH * (g + 1))
        u = _gelu(z_ref[:, Z_GU + GM_CH * g:Z_GU + GM_CH * (g + 1)])
        v = _rms(_gelu(z_ref[:, Z_GV + GM_CH * g:Z_GV + GM_CH * (g + 1)]), ggm_ref[:, sl]).astype(BF16)
        for c in range(tm // GM_CHUNK):
            rows = slice(GM_CHUNK * c, GM_CHUNK * (c + 1))
            mixed = jnp.dot(wsp_ref[g], v[rows], preferred_element_type=F32) + bsp_ref[g]
            gated_ref[rows, sl] = (u[rows] * mixed).astype(BF16)

    cq = _rms(z_ref[:, Z_CQ:Z_CKV], gq_ref[...]).astype(BF16)
    qm = jnp.dot(cq, wuq_ref[...], preferred_element_type=F32)
    for h in range(MLA_HEADS):
        lo = MLA_QH * h
        qm_ref[:, lo:lo + LANE] = (qm[:, lo:lo + LANE] * (MLA_SCALE * LOG2E)).astype(BF16)
        qm_ref[:, lo + LANE:lo + MLA_QH] = (
            _rope128(qm[:, lo + LANE:lo + MLA_QH], cos, sin) * (MLA_SCALE * LOG2E)).astype(BF16)

    ckv = _rms(z_ref[:, Z_CKV:Z_KR], gkv_ref[...]).astype(BF16)
    kv = jnp.dot(ckv, wukv_ref[...], preferred_element_type=F32)
    kn_ref[...] = kv[:, :MLA_HEADS * MLA_NOPE].astype(BF16)
    vm_ref[...] = kv[:, MLA_HEADS * MLA_NOPE:].astype(BF16)


def _nt_dot(a, b):
    return lax.dot_general(a, b, (((1,), (1,)), ((), ())), preferred_element_type=F32)


def _softmax_pieces(scores):
    m = functools.reduce(jnp.maximum, [jnp.max(s, axis=-1, keepdims=True) for s in scores])
    es = [jnp.exp2(s - m) for s in scores]
    total = functools.reduce(jnp.add, [jnp.sum(e, axis=-1, keepdims=True) for e in es])
    return es, 1.0 / total


def _da_attn_kernel(lam_ref, g_ref, q_ref, *refs, n_pieces, lam_init, sub):
    k_refs, v_refs, o_ref = refs[:n_pieces], refs[n_pieces:2 * n_pieces], refs[2 * n_pieces]
    lv = lam_ref[...]
    lam = (jnp.exp(jnp.sum(lv[0:1] * lv[1:2], axis=-1, keepdims=True))
           - jnp.exp(jnp.sum(lv[2:3] * lv[3:4], axis=-1, keepdims=True)) + lam_init)
    def scores(r):
        q = q_ref[sub * r:sub * (r + 1), :]
        lane = lax.broadcasted_iota(jnp.int32, q.shape, 1)
        zero = jnp.zeros_like(q)
        q0 = jnp.where(lane < DA_QK_DIM, q, zero)
        q1 = jnp.where(lane >= DA_QK_DIM, q, zero)
        return [_nt_dot(q0, k[...]) for k in k_refs], [_nt_dot(q1, k[...]) for k in k_refs]

    def finish(r, s0, s1):
        e0, r0 = _softmax_pieces(s0)
        e1, r1 = _softmax_pieces(s1)
        c = lam * r1 / r0
        o = r0 * functools.reduce(jnp.add, [
            jnp.dot((a - c * b).astype(BF16), v[...], preferred_element_type=F32)
            for a, b, v in zip(e0, e1, v_refs)])
        o_ref[sub * r:sub * (r + 1), :] = (_rms(o, g_ref[...]) * (1.0 - lam_init)).astype(BF16)

    _skewed(q_ref.shape[0] // sub, scores, finish)


def _skewed(n, first, second):
    pending = first(0)
    for r in range(n):
        ahead = first(r + 1) if r + 1 < n else None
        second(r, *pending)
        pending = ahead


def _mla_attn_kernel(q_ref, *refs, n_pieces, sub):
    kn_refs, kr_refs = refs[:n_pieces], refs[n_pieces:2 * n_pieces]
    vm_refs, o_ref = refs[2 * n_pieces:3 * n_pieces], refs[3 * n_pieces]

    def scores(r):
        qn, qr = q_ref[sub * r:sub * (r + 1), :LANE], q_ref[sub * r:sub * (r + 1), LANE:]
        return ([_nt_dot(qn, kn[...]) + _nt_dot(qr, kr[...]) for kn, kr in zip(kn_refs, kr_refs)],)

    def finish(r, s):
        es, rs = _softmax_pieces(s)
        o = rs * functools.reduce(jnp.add, [
            jnp.dot(e.astype(BF16), v[...], preferred_element_type=F32) for e, v in zip(es, vm_refs)])
        o_ref[sub * r:sub * (r + 1), :] = o.astype(BF16)

    _skewed(q_ref.shape[0] // sub, scores, finish)


def _outproj_kernel(x_ref, gt_ref, a_ref, b_ref, c_ref, wa_ref, wb_ref, wc_ref, o_ref):
    acc = (jnp.dot(a_ref[...], wa_ref[...], preferred_element_type=F32)
           + jnp.dot(b_ref[...], wb_ref[...], preferred_element_type=F32)
           + jnp.dot(c_ref[...], wc_ref[...], preferred_element_type=F32))
    o_ref[...] = x_ref[...] + gt_ref[0] * acc


def _mlp_kernel(x_ref, g_ref, sh_ref, sc_ref, gt_ref, gf_ref, w1_ref, w2_ref, o_ref, h_ref, acc_ref, *, final_norm):
    f = pl.program_id(1)

    @pl.when(f == 0)
    def _():
        y = _rms(x_ref[...], g_ref[...])
        h_ref[...] = (y * (1.0 + sc_ref[0]) + sh_ref[0]).astype(BF16)
        acc_ref[...] = jnp.zeros_like(acc_ref)

    a = jnp.maximum(jnp.dot(h_ref[...], w1_ref[...], preferred_element_type=F32), 0.0)
    acc_ref[...] += jnp.dot((a * a).astype(BF16), w2_ref[...], preferred_element_type=F32)

    @pl.when(f == pl.num_programs(1) - 1)
    def _():
        out = x_ref[...] + gt_ref[0] * acc_ref[...]
        if final_norm:
            out = _rms(out, gf_ref[...])
        o_ref[...] = out


def _mod_spec(row_of, chunk):
    return pl.BlockSpec((1, 1, D_MODEL), lambda i, *_: (row_of(i), 0, chunk))


def _mod_call(cvec, w_mod, b_mod):
    tn = 1024
    return pl.pallas_call(
        _mod_kernel,
        grid=(DEPTH, 6 * D_MODEL // tn),
        in_specs=[pl.BlockSpec((MOD_ROWS, D_MODEL), lambda l, n: (0, 0)),
                  pl.BlockSpec((1, D_MODEL, tn), lambda l, n: (l, 0, n)),
                  pl.BlockSpec((1, 1, tn), lambda l, n: (l, 0, n))],
        out_specs=pl.BlockSpec((1, MOD_ROWS, tn), lambda l, n: (l, 0, n)),
        out_shape=jax.ShapeDtypeStruct((DEPTH, MOD_ROWS, 6 * D_MODEL), F32),
        compiler_params=_cparams(("arbitrary", "arbitrary")),
        name="mod",
    )(cvec, w_mod, b_mod.reshape(DEPTH, 1, 6 * D_MODEL))


def _inproj_call(x, g, mod, row_of, w):
    m = x.shape[0]
    tm, tn = 512, 640
    return pl.pallas_call(
        _inproj_kernel,
        grid=(m // tm, Z_COLS // tn),
        in_specs=[pl.BlockSpec((tm, D_MODEL), lambda i, n: (i, 0)),
                  pl.BlockSpec((1, D_MODEL), lambda i, n: (0, 0)),
                  _mod_spec(row_of, 0), _mod_spec(row_of, 1),
                  pl.BlockSpec((D_MODEL, tn), lambda i, n: (0, n))],
        out_specs=pl.BlockSpec((tm, tn), lambda i, n: (i, n)),
        out_shape=jax.ShapeDtypeStruct((m, Z_COLS), F32),
        scratch_shapes=[pltpu.VMEM((tm, D_MODEL), BF16)],
        compiler_params=_cparams(("parallel", "arbitrary")),
        name="inproj",
    )(x, g, mod, mod, w)


def _zpost_call(z, cos_t, sin_t, ggm, wsp, bsp, gq, wuq, gkv, wukv):
    m = z.shape[0]
    tm = 256
    pos_tiles = cos_t.shape[0] // tm
    row = lambda i: (i, 0)
    const2 = lambda i: (0, 0)
    const3 = lambda i: (0, 0, 0)
    widths = (DA_QK_COLS, DA_QK_COLS, DA_WIDTH, GM_WIDTH, MLA_HEADS * MLA_QH,
              MLA_HEADS * MLA_NOPE, MLA_WIDTH, LANE)
    return pl.pallas_call(
        _zpost_kernel,
        grid=(m // tm,),
        in_specs=[pl.BlockSpec((tm, Z_COLS), row),
                  pl.BlockSpec((tm, LANE), lambda i: (i % pos_tiles, 0)),
                  pl.BlockSpec((tm, LANE), lambda i: (i % pos_tiles, 0)),
                  pl.BlockSpec((1, GM_WIDTH), const2),
                  pl.BlockSpec((GM_GROUPS, GM_CHUNK, GM_CHUNK), const3),
                  pl.BlockSpec((GM_GROUPS, GM_CHUNK, GM_CH), const3),
                  pl.BlockSpec((1, MLA_Q_RANK), const2),
                  pl.BlockSpec((MLA_Q_RANK, MLA_HEADS * MLA_QH), const2),
                  pl.BlockSpec((1, MLA_KV_RANK), const2),
                  pl.BlockSpec((MLA_KV_RANK, MLA_HEADS * (MLA_NOPE + MLA_V)), const2)],
        out_specs=[pl.BlockSpec((tm, w), row) for w in widths],
        out_shape=[jax.ShapeDtypeStruct((m, w), BF16) for w in widths],
        compiler_params=_cparams(("parallel",)),
        name="zpost",
    )(z, cos_t, sin_t, ggm, wsp, bsp, gq, wuq, gkv, wukv)


def _kv_specs(arrays, width):
    return [pl.BlockSpec((a.shape[0] // BATCH, width), lambda b, h, t: (b, h)) for a in arrays]


def _da_attn_call(lamv, g, q, ks, vs, lam_init):
    nq = q.shape[0] // BATCH
    tq = min(ATTN_TQ, nq)
    nqt = nq // tq
    qspec = pl.BlockSpec((tq, LANE), lambda b, h, t: (b * nqt + t, h))
    return pl.pallas_call(
        functools.partial(_da_attn_kernel, n_pieces=len(ks), lam_init=lam_init, sub=min(ATTN_SUB, tq)),
        grid=(BATCH, DA_HEADS, nqt),
        in_specs=[pl.BlockSpec(lamv.shape, lambda b, h, t: (0, 0)),
                  pl.BlockSpec((1, DA_V_DIM), lambda b, h, t: (0, 0)),
                  qspec] + _kv_specs(ks, LANE) + _kv_specs(vs, LANE),
        out_specs=qspec,
        out_shape=jax.ShapeDtypeStruct((q.shape[0], DA_WIDTH), BF16),
        compiler_params=_cparams(("parallel", "parallel", "arbitrary")),
        name="da_attn",
    )(lamv, g, q, *ks, *vs)


def _mla_attn_call(qm, kns, krs, vms):
    nq = qm.shape[0] // BATCH
    tq = min(ATTN_TQ, nq)
    nqt = nq // tq
    kr_specs = [pl.BlockSpec((a.shape[0] // BATCH, LANE), lambda b, h, t: (b, 0)) for a in krs]
    return pl.pallas_call(
        functools.partial(_mla_attn_kernel, n_pieces=len(kns), sub=min(ATTN_SUB, tq)),
        grid=(BATCH, MLA_HEADS, nqt),
        in_specs=[pl.BlockSpec((tq, MLA_QH), lambda b, h, t: (b * nqt + t, h))]
        + _kv_specs(kns, LANE) + kr_specs + _kv_specs(vms, LANE),
        out_specs=pl.BlockSpec((tq, MLA_V), lambda b, h, t: (b * nqt + t, h)),
        out_shape=jax.ShapeDtypeStruct((qm.shape[0], MLA_WIDTH), BF16),
        compiler_params=_cparams(("parallel", "parallel", "arbitrary")),
        name="mla_attn",
    )(qm, *kns, *krs, *vms)


def _outproj_call(x, mod, row_of, o_da, gated, o_mla, wa, wb, wc):
    m = x.shape[0]
    tm = 256
    row = lambda i: (i, 0)
    const = lambda i: (0, 0)
    return pl.pallas_call(
        _outproj_kernel,
        grid=(m // tm,),
        in_specs=[pl.BlockSpec((tm, D_MODEL), row), _mod_spec(row_of, 2),
                  pl.BlockSpec((tm, DA_WIDTH), row), pl.BlockSpec((tm, GM_WIDTH), row),
                  pl.BlockSpec((tm, MLA_WIDTH), row),
                  pl.BlockSpec((DA_WIDTH, D_MODEL), const), pl.BlockSpec((GM_WIDTH, D_MODEL), const),
                  pl.BlockSpec((MLA_WIDTH, D_MODEL), const)],
        out_specs=pl.BlockSpec((tm, D_MODEL), row),
        out_shape=jax.ShapeDtypeStruct((m, D_MODEL), F32),
        compiler_params=_cparams(("parallel",)),
        name="outproj",
    )(x, mod, o_da, gated, o_mla, wa, wb, wc)


def _mlp_call(x, g, mod, row_of, g_final, w1, w2, final_norm):
    m = x.shape[0]
    tm, tf = 512, 512
    row = lambda i, f: (i, 0)
    return pl.pallas_call(
        functools.partial(_mlp_kernel, final_norm=final_norm),
        grid=(m // tm, D_FF // tf),
        in_specs=[pl.BlockSpec((tm, D_MODEL), row),
                  pl.BlockSpec((1, D_MODEL), lambda i, f: (0, 0)),
                  _mod_spec(row_of, 3), _mod_spec(row_of, 4), _mod_spec(row_of, 5),
                  pl.BlockSpec((1, D_MODEL), lambda i, f: (0, 0)),
                  pl.BlockSpec((D_MODEL, tf), lambda i, f: (0, f)),
                  pl.BlockSpec((tf, D_MODEL), lambda i, f: (f, 0))],
        out_specs=pl.BlockSpec((tm, D_MODEL), row),
        out_shape=jax.ShapeDtypeStruct((m, D_MODEL), F32),
        scratch_shapes=[pltpu.VMEM((tm, D_MODEL), BF16), pltpu.VMEM((tm, D_MODEL), F32)],
        compiler_params=_cparams(("parallel", "arbitrary")),
        name="mlp",
    )(x, g, mod, mod, mod, g_final, w1, w2)


def kernel(x, c, ctx, c_ctx, w_mod, b_mod, g_norm_mix, g_norm_mlp, w_in, lam_q1, lam_k1, lam_q2, lam_k2,
           g_da_sub, g_gm_v, w_spatial, b_spatial, g_mla_q, w_mla_uq, g_mla_kv, w_mla_ukv, w_out, w_fc1,
           w_fc2, g_final):
    cos_t, sin_t = _rope_tables()
    cos_c = jnp.ones((CTX_LEN, LANE), F32)
    sin_c = jnp.zeros((CTX_LEN, LANE), F32)

    xl = x.reshape(BATCH * SEQ, D_MODEL)
    xc = ctx.reshape(BATCH * CTX_LEN, D_MODEL)
    cvec = jnp.zeros((MOD_ROWS, D_MODEL), F32).at[:BATCH].set(c).at[CTX_MOD_ROW].set(c_ctx)
    mod_all = _mod_call(cvec, w_mod, b_mod)

    lat_tiles = SEQ // 512
    lat_row_512 = lambda i: i // lat_tiles
    lat_row_256 = lambda i: i // (SEQ // 256)
    ctx_row = lambda i: CTX_MOD_ROW
    g_final2 = g_final.reshape(1, D_MODEL)

    for l in range(DEPTH):
        last = l == DEPTH - 1
        lam_init = 0.8 - 0.6 * math.exp(-0.3 * l)
        mod = mod_all[l].reshape(MOD_ROWS, 1, 6 * D_MODEL)
        w_in_p = _w_in_layout(w_in[l])
        wuq_p = _w_uq_layout(w_mla_uq[l])
        wukv_p = _w_ukv_layout(w_mla_ukv[l])
        w_out_b = w_out[l].astype(BF16)
        wa, wb, wc = w_out_b[:DA_WIDTH], w_out_b[DA_WIDTH:DA_WIDTH + GM_WIDTH], w_out_b[DA_WIDTH + GM_WIDTH:]
        w1, w2 = w_fc1[l].astype(BF16), w_fc2[l].astype(BF16)
        g_mix, g_mlp = g_norm_mix[l].reshape(1, D_MODEL), g_norm_mlp[l].reshape(1, D_MODEL)
        lamv = jnp.stack([lam_q1[l], lam_k1[l], lam_q2[l], lam_k2[l]])
        g_sub = g_da_sub[l].reshape(1, DA_V_DIM)
        post_w = (g_gm_v[l].reshape(1, GM_WIDTH), w_spatial[l].astype(BF16),
                  jnp.broadcast_to(b_spatial[l][:, :, None], (GM_GROUPS, GM_CHUNK, GM_CH)),
                  g_mla_q[l].reshape(1, MLA_Q_RANK), wuq_p, g_mla_kv[l].reshape(1, MLA_KV_RANK), wukv_p)

        z_l = _inproj_call(xl, g_mix, mod, lat_row_512, w_in_p)
        z_c = _inproj_call(xc, g_mix, mod, ctx_row, w_in_p)
        q, k, v, gated, qm, kn, vm, kr = _zpost_call(z_l, cos_t, sin_t, *post_w)
        qc, kc, vc, gated_c, qmc, knc, vmc, krc = _zpost_call(z_c, cos_c, sin_c, *post_w)

        o_da = _da_attn_call(lamv, g_sub, q, [k, kc], [v, vc], lam_init)
        o_mla = _mla_attn_call(qm, [kn, knc], [kr, krc], [vm, vmc])
        xl = _outproj_call(xl, mod, lat_row_256, o_da, gated, o_mla, wa, wb, wc)
        xl = _mlp_call(xl, g_mlp, mod, lat_row_512, g_final2, w1, w2, final_norm=last)

        if not last:
            o_da_c = _da_attn_call(lamv, g_sub, qc, [kc], [vc], lam_init)
            o_mla_c = _mla_attn_call(qmc, [knc], [krc], [vmc])
            xc = _outproj_call(xc, mod, ctx_row, o_da_c, gated_c, o_mla_c, wa, wb, wc)
            xc = _mlp_call(xc, g_mlp, mod, ctx_row, g_final2, w1, w2, final_norm=False)

    return xl.reshape(BATCH, SEQ, D_MODEL)
```

```python
import functools
import math

import numpy as np
import jax
import jax.numpy as jnp
from jax import lax
from jax.experimental import pallas as pl
from jax.experimental.pallas import tpu as pltpu

D_MODEL = 2048
BATCH = 4
SEQ = 2048
DEPTH = 2
CTX_LEN = 256
GRID_W = 64
EPS = 1e-6
ROPE_THETA = 10000.0

DA_HEADS = 6
DA_QK_DIM = 64
DA_V_DIM = 128
DA_WIDTH = DA_HEADS * DA_V_DIM
DA_QK_COLS = DA_HEADS * 2 * DA_QK_DIM
DA_SCALE = DA_QK_DIM ** -0.5

GM_GROUPS = 4
GM_CH = 128
GM_CHUNK = 128
GM_WIDTH = GM_GROUPS * GM_CH

MLA_HEADS = 6
MLA_Q_RANK = 512
MLA_KV_RANK = 512
MLA_NOPE = 128
MLA_ROPE = 64
MLA_V = 128
MLA_WIDTH = MLA_HEADS * MLA_V
MLA_SCALE = (MLA_NOPE + MLA_ROPE) ** -0.5
LOG2E = math.log2(math.e)

ROT_DIM = 64
D_FF = 4 * D_MODEL
MOD_ROWS = 8
CTX_MOD_ROW = BATCH

LANE = 128
Z_Q, Z_K, Z_V, Z_GU, Z_GV, Z_CQ, Z_CKV, Z_KR = 0, 768, 1536, 2304, 2816, 3328, 3840, 4352
Z_COLS = Z_KR + LANE
Z_PAD = 4608
NORM_ROWS = 16
NORM_UNROLL = 4
MLA_QH = 2 * LANE

VMEM_LIMIT = 56 * 1024 * 1024
ATTN_TQ = 1024
ATTN_SUB = 128

BF16 = jnp.bfloat16
F32 = jnp.float32


def _cparams(sem):
    return pltpu.CompilerParams(dimension_semantics=sem, vmem_limit_bytes=VMEM_LIMIT)


def _w_in_layout(w):
    return jnp.pad(w, ((0, 0), (0, Z_PAD - w.shape[1]))).astype(BF16)


def _w_uq_layout(w):
    k = w.shape[0]
    w = w.reshape(k, MLA_HEADS, MLA_NOPE + MLA_ROPE)
    return jnp.pad(w, ((0, 0), (0, 0), (0, MLA_QH - MLA_NOPE - MLA_ROPE))).reshape(k, -1).astype(BF16)


def _rope_tables():
    rows = SEQ // GRID_W
    row = jnp.repeat(jnp.arange(rows, dtype=F32), GRID_W)
    col = jnp.tile(jnp.arange(GRID_W, dtype=F32), rows)
    n_f = ROT_DIM // 4
    inv = ROPE_THETA ** (-jnp.arange(n_f, dtype=F32) / n_f)
    ang = jnp.concatenate([row[:, None] * inv, col[:, None] * inv], axis=-1)
    cos, sin = jnp.cos(ang), jnp.sin(ang)
    cos_u = jnp.repeat(cos, 2, axis=-1)
    sin_u = jnp.stack([-sin, sin], axis=-1).reshape(SEQ, ROT_DIM)
    return jnp.tile(cos_u, (1, 2)), jnp.tile(sin_u, (1, 2))


def _rms(x, g):
    return x * lax.rsqrt(jnp.mean(x * x, axis=-1, keepdims=True) + EPS) * g


def _gelu(x):
    return 0.5 * x * (1.0 + lax.erf(x * math.sqrt(0.5)))


def _rope128(x, cos, sin):
    lane = lax.broadcasted_iota(jnp.int32, x.shape, 1)
    partner = jnp.where((lane & 1) == 0, pltpu.roll(x, LANE - 1, 1), pltpu.roll(x, 1, 1))
    return x * cos + partner * sin


def _norm_mod_rows(x_ref, g_ref, sh_ref, sc_ref, h_ref):
    gain = g_ref[...] * (1.0 + sc_ref[0])
    shift = sh_ref[0]

    def body(i, carry):
        for rows in _chunk_rows(i):
            x = x_ref[rows, :]
            inv = lax.rsqrt(jnp.mean(x * x, axis=-1, keepdims=True) + EPS)
            h_ref[rows, :] = (x * inv * gain + shift).astype(BF16)
        return carry

    lax.fori_loop(0, x_ref.shape[0] // (NORM_ROWS * NORM_UNROLL), body, 0)


def _chunk_rows(i):
    base = pl.multiple_of(i * (NORM_ROWS * NORM_UNROLL), NORM_ROWS * NORM_UNROLL)
    return [pl.ds(base + NORM_ROWS * k, NORM_ROWS) for k in range(NORM_UNROLL)]


def _mod_kernel(c_ref, w_ref, b_ref, o_ref):
    c = c_ref[...]
    s = (c / (1.0 + jnp.exp(-c))).astype(BF16)
    o_ref[0] = jnp.dot(s, w_ref[0].astype(BF16), preferred_element_type=F32) + b_ref[0]


def _inproj_kernel(x_ref, g_ref, sh_ref, sc_ref, w_ref, z_ref, h_ref):
    @pl.when(pl.program_id(1) == 0)
    def _():
        _norm_mod_rows(x_ref, g_ref, sh_ref, sc_ref, h_ref)

    z_ref[...] = jnp.dot(h_ref[...], w_ref[...], preferred_element_type=F32)


def _zpost_kernel(z_ref, cos_ref, sin_ref, ggm_ref, wsp_ref, bsp_ref, gq_ref, wuq_ref, gkv_ref, wukv_ref,
                  q_ref, k_ref, v_ref, gated_ref, qm_ref, kn_ref, vm_ref, kr_ref):
    tm = z_ref.shape[0]
    cos, sin = cos_ref[...], sin_ref[...]
    for j in range(DA_HEADS):
        sl = slice(LANE * j, LANE * (j + 1))
        q_ref[:, sl] = (_rope128(z_ref[:, Z_Q + LANE * j:Z_Q + LANE * (j + 1)], cos, sin)
                        * (DA_SCALE * LOG2E)).astype(BF16)
        k_ref[:, sl] = _rope128(z_ref[:, Z_K + LANE * j:Z_K + LANE * (j + 1)], cos, sin).astype(BF16)
    v_ref[...] = z_ref[:, Z_V:Z_GU].astype(BF16)
    kr_ref[...] = _rope128(z_ref[:, Z_KR:Z_COLS], cos, sin).astype(BF16)

    for g in range(GM_GROUPS):
        sl = slice(GM_CH * g, GM_CH * (g + 1))
        u = _gelu(z_ref[:, Z_GU + GM_CH * g:Z_GU + GM_CH * (g + 1)])
        v = _rms(_gelu(z_ref[:, Z_GV + GM_CH * g:Z_GV + GM_CH * (g + 1)]), ggm_ref[:, sl]).astype(BF16)
        for c in range(tm // GM_CHUNK):
            rows = slice(GM_CHUNK * c, GM_CHUNK * (c + 1))
            mixed = jnp.dot(wsp_ref[g], v[rows], preferred_element_type=F32) + bsp_ref[g]
            gated_ref[rows, sl] = (u[rows] * mixed).astype(BF16)

    cq = _rms(z_ref[:, Z_CQ:Z_CKV], gq_ref[...]).astype(BF16)
    qm = jnp.dot(cq, wuq_ref[...], preferred_element_type=F32)
    for h in range(MLA_HEADS):
        lo = MLA_QH * h
        qm_ref[:, lo:lo + LANE] = (qm[:, lo:lo + LANE] * (MLA_SCALE * LOG2E)).astype(BF16)
        qm_ref[:, lo + LANE:lo + MLA_QH] = (
            _rope128(qm[:, lo + LANE:lo + MLA_QH], cos, sin) * (MLA_SCALE * LOG2E)).astype(BF16)

    ckv = _rms(z_ref[:, Z_CKV:Z_KR], gkv_ref[...]).astype(BF16)
    kv = jnp.dot(ckv, wukv_ref[...], preferred_element_type=F32)
    for h in range(MLA_HEADS):
        lo = (MLA_NOPE + MLA_V) * h
        kn_ref[:, MLA_NOPE * h:MLA_NOPE * (h + 1)] = kv[:, lo:lo + MLA_NOPE].astype(BF16)
        vm_ref[:, MLA_V * h:MLA_V * (h + 1)] = kv[:, lo + MLA_NOPE:lo + MLA_NOPE + MLA_V].astype(BF16)


def _nt_dot(a, b):
    return lax.dot_general(a, b, (((1,), (1,)), ((), ())), preferred_element_type=F32)


def _softmax_pieces(scores):
    m = functools.reduce(jnp.maximum, [jnp.max(s, axis=-1, keepdims=True) for s in scores])
    es = [jnp.exp2(s - m) for s in scores]
    total = functools.reduce(jnp.add, [jnp.sum(e, axis=-1, keepdims=True) for e in es])
    return es, 1.0 / total


def _da_attn_kernel(lam_ref, g_ref, q_ref, *refs, n_pieces, lam_init, sub):
    k_refs, v_refs, o_ref = refs[:n_pieces], refs[n_pieces:2 * n_pieces], refs[2 * n_pieces]
    lv = lam_ref[...]
    lam = (jnp.exp(jnp.sum(lv[0:1] * lv[1:2], axis=-1, keepdims=True))
           - jnp.exp(jnp.sum(lv[2:3] * lv[3:4], axis=-1, keepdims=True)) + lam_init)
    def scores(r):
        q = q_ref[sub * r:sub * (r + 1), :]
        lane = lax.broadcasted_iota(jnp.int32, q.shape, 1)
        zero = jnp.zeros_like(q)
        q0 = jnp.where(lane < DA_QK_DIM, q, zero)
        q1 = jnp.where(lane >= DA_QK_DIM, q, zero)
        return [_nt_dot(q0, k[...]) for k in k_refs], [_nt_dot(q1, k[...]) for k in k_refs]

    def finish(r, s0, s1):
        e0, r0 = _softmax_pieces(s0)
        e1, r1 = _softmax_pieces(s1)
        c = lam * r1 / r0
        o = r0 * functools.reduce(jnp.add, [
            jnp.dot((a - c * b).astype(BF16), v[...], preferred_element_type=F32)
            for a, b, v in zip(e0, e1, v_refs)])
        o_ref[sub * r:sub * (r + 1), :] = (_rms(o, g_ref[...]) * (1.0 - lam_init)).astype(BF16)

    _skewed(q_ref.shape[0] // sub, scores, finish)


def _skewed(n, first, second):
    pending = first(0)
    for r in range(n):
        ahead = first(r + 1) if r + 1 < n else None
        second(r, *pending)
        pending = ahead


def _mla_attn_kernel(q_ref, *refs, n_pieces, sub):
    kn_refs, kr_refs = refs[:n_pieces], refs[n_pieces:2 * n_pieces]
    vm_refs, o_ref = refs[2 * n_pieces:3 * n_pieces], refs[3 * n_pieces]

    def scores(r):
        qn, qr = q_ref[sub * r:sub * (r + 1), :LANE], q_ref[sub * r:sub * (r + 1), LANE:]
        return ([_nt_dot(qn, kn[...]) + _nt_dot(qr, kr[...]) for kn, kr in zip(kn_refs, kr_refs)],)

    def finish(r, s):
        es, rs = _softmax_pieces(s)
        o = rs * functools.reduce(jnp.add, [
            jnp.dot(e.astype(BF16), v[...], preferred_element_type=F32) for e, v in zip(es, vm_refs)])
        o_ref[sub * r:sub * (r + 1), :] = o.astype(BF16)

    _skewed(q_ref.shape[0] // sub, scores, finish)


def _outproj_kernel(x_ref, gt_ref, a_ref, b_ref, c_ref, wa_ref, wb_ref, wc_ref, o_ref):
    acc = (jnp.dot(a_ref[...], wa_ref[...], preferred_element_type=F32)
           + jnp.dot(b_ref[...], wb_ref[...], preferred_element_type=F32)
           + jnp.dot(c_ref[...], wc_ref[...], preferred_element_type=F32))
    o_ref[...] = x_ref[...] + gt_ref[0] * acc


def _mlp_kernel(x_ref, g_ref, sh_ref, sc_ref, gt_ref, gf_ref, w1_ref, w2_ref, o_ref, h_ref, *, final_norm):
    f = pl.program_id(1)

    @pl.when(f == 0)
    def _():
        _norm_mod_rows(x_ref, g_ref, sh_ref, sc_ref, h_ref)
        o_ref[...] = jnp.zeros_like(o_ref)

    a = jnp.maximum(jnp.dot(h_ref[...], w1_ref[...].astype(BF16), preferred_element_type=F32), 0.0)
    o_ref[...] += jnp.dot((a * a).astype(BF16), w2_ref[...].astype(BF16), preferred_element_type=F32)

    @pl.when(f == pl.num_programs(1) - 1)
    def _():
        gate, gfin = gt_ref[0], gf_ref[...]

        def body(i, carry):
            chunks = _chunk_rows(i)
            outs = [x_ref[rows, :] + gate * o_ref[rows, :] for rows in chunks]
            for rows, out in zip(chunks, outs):
                o_ref[rows, :] = _rms(out, gfin) if final_norm else out
            return carry

        lax.fori_loop(0, o_ref.shape[0] // (NORM_ROWS * NORM_UNROLL), body, 0)


LATENT_ROWS = (0, SEQ)
CONTEXT_ROWS = (CTX_MOD_ROW, None)


def _mod_spec(mod_rows, tm, chunk):
    first, rows_per = mod_rows
    if rows_per is None:
        return pl.BlockSpec((1, 1, D_MODEL), lambda i, *_: (first, 0, chunk))
    return pl.BlockSpec((1, 1, D_MODEL), lambda i, *_: (first + (i * tm) // rows_per, 0, chunk))


def _mod_call(cvec, w_mod, b_mod):
    tn = 1024
    return pl.pallas_call(
        _mod_kernel,
        grid=(DEPTH, 6 * D_MODEL // tn),
        in_specs=[pl.BlockSpec((MOD_ROWS, D_MODEL), lambda l, n: (0, 0)),
                  pl.BlockSpec((1, D_MODEL, tn), lambda l, n: (l, 0, n)),
                  pl.BlockSpec((1, 1, tn), lambda l, n: (l, 0, n))],
        out_specs=pl.BlockSpec((1, MOD_ROWS, tn), lambda l, n: (l, 0, n)),
        out_shape=jax.ShapeDtypeStruct((DEPTH, MOD_ROWS, 6 * D_MODEL), F32),
        compiler_params=_cparams(("arbitrary", "arbitrary")),
        name="mod",
    )(cvec, w_mod, b_mod.reshape(DEPTH, 1, 6 * D_MODEL))


def _inproj_call(x, g, mod, mod_rows, w):
    m = x.shape[0]
    tm, tn = 1024, 768
    return pl.pallas_call(
        _inproj_kernel,
        grid=(m // tm, Z_PAD // tn),
        in_specs=[pl.BlockSpec((tm, D_MODEL), lambda i, n: (i, 0)),
                  pl.BlockSpec((1, D_MODEL), lambda i, n: (0, 0)),
                  _mod_spec(mod_rows, tm, 0), _mod_spec(mod_rows, tm, 1),
                  pl.BlockSpec((D_MODEL, tn), lambda i, n: (0, n))],
        out_specs=pl.BlockSpec((tm, tn), lambda i, n: (i, n)),
        out_shape=jax.ShapeDtypeStruct((m, Z_PAD), F32),
        scratch_shapes=[pltpu.VMEM((tm, D_MODEL), BF16)],
        compiler_params=_cparams(("parallel", "arbitrary")),
        name="inproj",
    )(x, g, mod, mod, w)


def _zpost_call(z, cos_t, sin_t, ggm, wsp, bsp, gq, wuq, gkv, wukv):
    m = z.shape[0]
    tm = 256
    pos_tiles = cos_t.shape[0] // tm
    row = lambda i: (i, 0)
    const2 = lambda i: (0, 0)
    const3 = lambda i: (0, 0, 0)
    widths = (DA_QK_COLS, DA_QK_COLS, DA_WIDTH, GM_WIDTH, MLA_HEADS * MLA_QH,
              MLA_HEADS * MLA_NOPE, MLA_WIDTH, LANE)
    return pl.pallas_call(
        _zpost_kernel,
        grid=(m // tm,),
        in_specs=[pl.BlockSpec((tm, Z_COLS), row),
                  pl.BlockSpec((tm, LANE), lambda i: (i % pos_tiles, 0)),
                  pl.BlockSpec((tm, LANE), lambda i: (i % pos_tiles, 0)),
                  pl.BlockSpec((1, GM_WIDTH), const2),
                  pl.BlockSpec((GM_GROUPS, GM_CHUNK, GM_CHUNK), const3),
                  pl.BlockSpec((GM_GROUPS, GM_CHUNK, GM_CH), const3),
                  pl.BlockSpec((1, MLA_Q_RANK), const2),
                  pl.BlockSpec((MLA_Q_RANK, MLA_HEADS * MLA_QH), const2),
                  pl.BlockSpec((1, MLA_KV_RANK), const2),
                  pl.BlockSpec((MLA_KV_RANK, MLA_HEADS * (MLA_NOPE + MLA_V)), const2)],
        out_specs=[pl.BlockSpec((tm, w), row) for w in widths],
        out_shape=[jax.ShapeDtypeStruct((m, w), BF16) for w in widths],
        compiler_params=_cparams(("parallel",)),
        name="zpost",
    )(z, cos_t, sin_t, ggm, wsp, bsp, gq, wuq, gkv, wukv)


def _kv_specs(arrays, width):
    return [pl.BlockSpec((a.shape[0] // BATCH, width), lambda b, h, t: (b, h)) for a in arrays]


def _da_attn_call(lamv, g, q, ks, vs, lam_init):
    nq = q.shape[0] // BATCH
    tq = min(ATTN_TQ, nq)
    nqt = nq // tq
    qspec = pl.BlockSpec((tq, LANE), lambda b, h, t: (b * nqt + t, h))
    return pl.pallas_call(
        functools.partial(_da_attn_kernel, n_pieces=len(ks), lam_init=lam_init, sub=min(ATTN_SUB, tq)),
        grid=(BATCH, DA_HEADS, nqt),
        in_specs=[pl.BlockSpec(lamv.shape, lambda b, h, t: (0, 0)),
                  pl.BlockSpec((1, DA_V_DIM), lambda b, h, t: (0, 0)),
                  qspec] + _kv_specs(ks, LANE) + _kv_specs(vs, LANE),
        out_specs=qspec,
        out_shape=jax.ShapeDtypeStruct((q.shape[0], DA_WIDTH), BF16),
        compiler_params=_cparams(("parallel", "parallel", "arbitrary")),
        name="da_attn",
    )(lamv, g, q, *ks, *vs)


def _mla_attn_call(qm, kns, krs, vms):
    nq = qm.shape[0] // BATCH
    tq = min(ATTN_TQ, nq)
    nqt = nq // tq
    kr_specs = [pl.BlockSpec((a.shape[0] // BATCH, LANE), lambda b, h, t: (b, 0)) for a in krs]
    return pl.pallas_call(
        functools.partial(_mla_attn_kernel, n_pieces=len(kns), sub=min(ATTN_SUB, tq)),
        grid=(BATCH, MLA_HEADS, nqt),
        in_specs=[pl.BlockSpec((tq, MLA_QH), lambda b, h, t: (b * nqt + t, h))]
        + _kv_specs(kns, LANE) + kr_specs + _kv_specs(vms, LANE),
        out_specs=pl.BlockSpec((tq, MLA_V), lambda b, h, t: (b * nqt + t, h)),
        out_shape=jax.ShapeDtypeStruct((qm.shape[0], MLA_WIDTH), BF16),
        compiler_params=_cparams(("parallel", "parallel", "arbitrary")),
        name="mla_attn",
    )(qm, *kns, *krs, *vms)


def _outproj_call(x, mod, mod_rows, o_da, gated, o_mla, wa, wb, wc):
    m = x.shape[0]
    tm = 256
    row = lambda i: (i, 0)
    const = lambda i: (0, 0)
    return pl.pallas_call(
        _outproj_kernel,
        grid=(m // tm,),
        in_specs=[pl.BlockSpec((tm, D_MODEL), row), _mod_spec(mod_rows, tm, 2),
                  pl.BlockSpec((tm, DA_WIDTH), row), pl.BlockSpec((tm, GM_WIDTH), row),
                  pl.BlockSpec((tm, MLA_WIDTH), row),
                  pl.BlockSpec((DA_WIDTH, D_MODEL), const), pl.BlockSpec((GM_WIDTH, D_MODEL), const),
                  pl.BlockSpec((MLA_WIDTH, D_MODEL), const)],
        out_specs=pl.BlockSpec((tm, D_MODEL), row),
        out_shape=jax.ShapeDtypeStruct((m, D_MODEL), F32),
        compiler_params=_cparams(("parallel",)),
        name="outproj",
    )(x, mod, o_da, gated, o_mla, wa, wb, wc)


def _mlp_call(x, g, mod, mod_rows, g_final, w1, w2, final_norm):
    m = x.shape[0]
    tm, tf = 1024, 512
    row = lambda i, f: (i, 0)
    return pl.pallas_call(
        functools.partial(_mlp_kernel, final_norm=final_norm),
        grid=(m // tm, D_FF // tf),
        in_specs=[pl.BlockSpec((tm, D_MODEL), row, pipeline_mode=pl.Buffered(1)),
                  pl.BlockSpec((1, D_MODEL), lambda i, f: (0, 0)),
                  _mod_spec(mod_rows, tm, 3), _mod_spec(mod_rows, tm, 4), _mod_spec(mod_rows, tm, 5),
                  pl.BlockSpec((1, D_MODEL), lambda i, f: (0, 0)),
                  pl.BlockSpec((D_MODEL, tf), lambda i, f: (0, f)),
                  pl.BlockSpec((tf, D_MODEL), lambda i, f: (f, 0))],
        out_specs=pl.BlockSpec((tm, D_MODEL), row),
        out_shape=jax.ShapeDtypeStruct((m, D_MODEL), F32),
        scratch_shapes=[pltpu.VMEM((tm, D_MODEL), BF16)],
        compiler_params=_cparams(("parallel", "arbitrary")),
        name="mlp",
    )(x, g, mod, mod, mod, g_final, w1, w2)


def kernel(x, c, ctx, c_ctx, w_mod, b_mod, g_norm_mix, g_norm_mlp, w_in, lam_q1, lam_k1, lam_q2, lam_k2,
           g_da_sub, g_gm_v, w_spatial, b_spatial, g_mla_q, w_mla_uq, g_mla_kv, w_mla_ukv, w_out, w_fc1,
           w_fc2, g_final):
    cos_t, sin_t = _rope_tables()
    cos_c = jnp.ones((CTX_LEN, LANE), F32)
    sin_c = jnp.zeros((CTX_LEN, LANE), F32)

    xl = x.reshape(BATCH * SEQ, D_MODEL)
    xc = ctx.reshape(BATCH * CTX_LEN, D_MODEL)
    cvec = jnp.zeros((MOD_ROWS, D_MODEL), F32).at[:BATCH].set(c).at[CTX_MOD_ROW].set(c_ctx)
    mod_all = _mod_call(cvec, w_mod, b_mod)

    g_final2 = g_final.reshape(1, D_MODEL)

    for l in range(DEPTH):
        last = l == DEPTH - 1
        lam_init = 0.8 - 0.6 * math.exp(-0.3 * l)
        mod = mod_all[l].reshape(MOD_ROWS, 1, 6 * D_MODEL)
        w_in_p = _w_in_layout(w_in[l])
        wuq_p = _w_uq_layout(w_mla_uq[l])
        wukv_p = w_mla_ukv[l].astype(BF16)
        w_out_b = w_out[l].astype(BF16)
        wa, wb, wc = w_out_b[:DA_WIDTH], w_out_b[DA_WIDTH:DA_WIDTH + GM_WIDTH], w_out_b[DA_WIDTH + GM_WIDTH:]
        w1, w2 = w_fc1[l], w_fc2[l]
        g_mix, g_mlp = g_norm_mix[l].reshape(1, D_MODEL), g_norm_mlp[l].reshape(1, D_MODEL)
        lamv = jnp.stack([lam_q1[l], lam_k1[l], lam_q2[l], lam_k2[l]])
        g_sub = g_da_sub[l].reshape(1, DA_V_DIM)
        post_w = (g_gm_v[l].reshape(1, GM_WIDTH), w_spatial[l].astype(BF16),
                  jnp.broadcast_to(b_spatial[l][:, :, None], (GM_GROUPS, GM_CHUNK, GM_CH)),
                  g_mla_q[l].reshape(1, MLA_Q_RANK), wuq_p, g_mla_kv[l].reshape(1, MLA_KV_RANK), wukv_p)

        z_l = _inproj_call(xl, g_mix, mod, LATENT_ROWS, w_in_p)
        z_c = _inproj_call(xc, g_mix, mod, CONTEXT_ROWS, w_in_p)
        q, k, v, gated, qm, kn, vm, kr = _zpost_call(z_l, cos_t, sin_t, *post_w)
        qc, kc, vc, gated_c, qmc, knc, vmc, krc = _zpost_call(z_c, cos_c, sin_c, *post_w)

        o_da = _da_attn_call(lamv, g_sub, q, [k, kc], [v, vc], lam_init)
        o_mla = _mla_attn_call(qm, [kn, knc], [kr, krc], [vm, vmc])
        xl = _outproj_call(xl, mod, LATENT_ROWS, o_da, gated, o_mla, wa, wb, wc)
        xl = _mlp_call(xl, g_mlp, mod, LATENT_ROWS, g_final2, w1, w2, final_norm=last)

        if not last:
            o_da_c = _da_attn_call(lamv, g_sub, qc, [kc], [vc], lam_init)
            o_mla_c = _mla_attn_call(qmc, [knc], [krc], [vmc])
            xc = _outproj_call(xc, mod, CONTEXT_ROWS, o_da_c, gated_c, o_mla_c, wa, wb, wc)
            xc = _mlp_call(xc, g_mlp, mod, CONTEXT_ROWS, g_final2, w1, w2, final_norm=False)

    return xl.reshape(BATCH, SEQ, D_MODEL)
```

```python
import functools
import math

import numpy as np
import jax
import jax.numpy as jnp
from jax import lax
from jax.experimental import pallas as pl
from jax.experimental.pallas import tpu as pltpu

D_MODEL = 2048
BATCH = 4
SEQ = 2048
DEPTH = 2
CTX_LEN = 256
GRID_W = 64
EPS = 1e-6
ROPE_THETA = 10000.0

DA_HEADS = 6
DA_QK_DIM = 64
DA_V_DIM = 128
DA_WIDTH = DA_HEADS * DA_V_DIM
DA_QK_COLS = DA_HEADS * 2 * DA_QK_DIM
DA_SCALE = DA_QK_DIM ** -0.5

GM_GROUPS = 4
GM_CH = 128
GM_CHUNK = 128
GM_WIDTH = GM_GROUPS * GM_CH

MLA_HEADS = 6
MLA_Q_RANK = 512
MLA_KV_RANK = 512
MLA_NOPE = 128
MLA_ROPE = 64
MLA_V = 128
MLA_WIDTH = MLA_HEADS * MLA_V
MIX_WIDTH = DA_WIDTH + GM_WIDTH + MLA_WIDTH
MLA_SCALE = (MLA_NOPE + MLA_ROPE) ** -0.5
LOG2E = math.log2(math.e)

ROT_DIM = 64
D_FF = 4 * D_MODEL
MOD_ROWS = 8
CTX_MOD_ROW = BATCH

LANE = 128
Z_Q, Z_K, Z_V, Z_GU, Z_GV, Z_CQ, Z_CKV, Z_KR = 0, 768, 1536, 2304, 2816, 3328, 3840, 4352
Z_COLS = Z_KR + LANE
Z_PAD = 4608
NORM_ROWS = 16
NORM_UNROLL = 4
MLA_QH = 2 * LANE

VMEM_LIMIT = 56 * 1024 * 1024
ATTN_TQ = 1024
ATTN_SUB = 128

BF16 = jnp.bfloat16
F32 = jnp.float32


def _cparams(sem):
    return pltpu.CompilerParams(dimension_semantics=sem, vmem_limit_bytes=VMEM_LIMIT)


def _w_in_layout(w):
    return jnp.pad(w, ((0, 0), (0, 0), (0, Z_PAD - w.shape[-1]))).astype(BF16)


def _w_uq_layout(w):
    k = w.shape[0]
    w = w.reshape(k, MLA_HEADS, MLA_NOPE + MLA_ROPE)
    return jnp.pad(w, ((0, 0), (0, 0), (0, MLA_QH - MLA_NOPE - MLA_ROPE))).reshape(k, -1).astype(BF16)


def _rope_tables():
    rows = SEQ // GRID_W
    row = jnp.repeat(jnp.arange(rows, dtype=F32), GRID_W)
    col = jnp.tile(jnp.arange(GRID_W, dtype=F32), rows)
    n_f = ROT_DIM // 4
    inv = ROPE_THETA ** (-jnp.arange(n_f, dtype=F32) / n_f)
    ang = jnp.concatenate([row[:, None] * inv, col[:, None] * inv], axis=-1)
    cos, sin = jnp.cos(ang), jnp.sin(ang)
    cos_u = jnp.repeat(cos, 2, axis=-1)
    sin_u = jnp.stack([-sin, sin], axis=-1).reshape(SEQ, ROT_DIM)
    return jnp.tile(cos_u, (1, 2)), jnp.tile(sin_u, (1, 2))


def _rms(x, g):
    return x * lax.rsqrt(jnp.mean(x * x, axis=-1, keepdims=True) + EPS) * g


def _gelu(x):
    return 0.5 * x * (1.0 + lax.erf(x * math.sqrt(0.5)))


def _rope128(x, cos, sin):
    lane = lax.broadcasted_iota(jnp.int32, x.shape, 1)
    partner = jnp.where((lane & 1) == 0, pltpu.roll(x, LANE - 1, 1), pltpu.roll(x, 1, 1))
    return x * cos + partner * sin


def _norm_mod_rows(x_ref, g_ref, sh_ref, sc_ref, h_ref):
    gain = g_ref[...] * (1.0 + sc_ref[0])
    shift = sh_ref[0]

    def body(i, carry):
        for rows in _chunk_rows(i):
            x = x_ref[rows, :]
            inv = lax.rsqrt(jnp.mean(x * x, axis=-1, keepdims=True) + EPS)
            h_ref[rows, :] = (x * inv * gain + shift).astype(BF16)
        return carry

    lax.fori_loop(0, x_ref.shape[0] // (NORM_ROWS * NORM_UNROLL), body, 0)


def _chunk_rows(i):
    base = pl.multiple_of(i * (NORM_ROWS * NORM_UNROLL), NORM_ROWS * NORM_UNROLL)
    return [pl.ds(base + NORM_ROWS * k, NORM_ROWS) for k in range(NORM_UNROLL)]


def _mod_kernel(c_ref, w_ref, b_ref, o_ref):
    c = c_ref[...]
    s = (c / (1.0 + jnp.exp(-c))).astype(BF16)
    o_ref[0] = jnp.dot(s, w_ref[0].astype(BF16), preferred_element_type=F32) + b_ref[0]


def _inproj_kernel(x_ref, g_ref, sh_ref, sc_ref, w_ref, z_ref, h_ref):
    @pl.when(pl.program_id(1) == 0)
    def _():
        _norm_mod_rows(x_ref, g_ref, sh_ref, sc_ref, h_ref)

    z_ref[...] = jnp.dot(h_ref[...], w_ref[...], preferred_element_type=F32)


def _zpost_kernel(z_ref, cos_ref, sin_ref, ggm_ref, wsp_ref, bsp_ref, gq_ref, wuq_ref, gkv_ref, wukv_ref,
                  q_ref, k_ref, v_ref, gated_ref, qm_ref, kn_ref, vm_ref, kr_ref):
    tm = z_ref.shape[0]
    cos, sin = cos_ref[...], sin_ref[...]
    for j in range(DA_HEADS):
        sl = slice(LANE * j, LANE * (j + 1))
        q_ref[:, sl] = (_rope128(z_ref[:, Z_Q + LANE * j:Z_Q + LANE * (j + 1)], cos, sin)
                        * (DA_SCALE * LOG2E)).astype(BF16)
        k_ref[:, sl] = _rope128(z_ref[:, Z_K + LANE * j:Z_K + LANE * (j + 1)], cos, sin).astype(BF16)
    v_ref[...] = z_ref[:, Z_V:Z_GU].astype(BF16)
    kr_ref[...] = _rope128(z_ref[:, Z_KR:Z_COLS], cos, sin).astype(BF16)

    for g in range(GM_GROUPS):
        sl = slice(GM_CH * g, GM_CH * (g + 1))
        u = _gelu(z_ref[:, Z_GU + GM_CH * g:Z_GU + GM_CH * (g + 1)])
        v = _rms(_gelu(z_ref[:, Z_GV + GM_CH * g:Z_GV + GM_CH * (g + 1)]), ggm_ref[:, sl]).astype(BF16)
        for c in range(tm // GM_CHUNK):
            rows = slice(GM_CHUNK * c, GM_CHUNK * (c + 1))
            mixed = jnp.dot(wsp_ref[g], v[rows], preferred_element_type=F32) + bsp_ref[g]
            gated_ref[rows, sl] = (u[rows] * mixed).astype(BF16)

    cq = _rms(z_ref[:, Z_CQ:Z_CKV], gq_ref[...]).astype(BF16)
    qm = jnp.dot(cq, wuq_ref[...], preferred_element_type=F32)
    for h in range(MLA_HEADS):
        lo = MLA_QH * h
        qm_ref[:, lo:lo + LANE] = (qm[:, lo:lo + LANE] * (MLA_SCALE * LOG2E)).astype(BF16)
        qm_ref[:, lo + LANE:lo + MLA_QH] = (
            _rope128(qm[:, lo + LANE:lo + MLA_QH], cos, sin) * (MLA_SCALE * LOG2E)).astype(BF16)

    ckv = _rms(z_ref[:, Z_CKV:Z_KR], gkv_ref[...]).astype(BF16)
    kv = jnp.dot(ckv, wukv_ref[...], preferred_element_type=F32)
    for h in range(MLA_HEADS):
        lo = (MLA_NOPE + MLA_V) * h
        kn_ref[:, MLA_NOPE * h:MLA_NOPE * (h + 1)] = kv[:, lo:lo + MLA_NOPE].astype(BF16)
        vm_ref[:, MLA_V * h:MLA_V * (h + 1)] = kv[:, lo + MLA_NOPE:lo + MLA_NOPE + MLA_V].astype(BF16)


def _nt_dot(a, b):
    return lax.dot_general(a, b, (((1,), (1,)), ((), ())), preferred_element_type=F32)


def _softmax_pieces(scores):
    m = functools.reduce(jnp.maximum, [jnp.max(s, axis=-1, keepdims=True) for s in scores])
    es = [jnp.exp2(s - m) for s in scores]
    total = functools.reduce(jnp.add, [jnp.sum(e, axis=-1, keepdims=True) for e in es])
    return es, 1.0 / total


def _da_attn_kernel(lam_ref, g_ref, q_ref, *refs, n_pieces, lam_init, sub):
    k_refs, v_refs, o_ref = refs[:n_pieces], refs[n_pieces:2 * n_pieces], refs[2 * n_pieces]
    lv = lam_ref[...]
    lam = (jnp.exp(jnp.sum(lv[0:1] * lv[1:2], axis=-1, keepdims=True))
           - jnp.exp(jnp.sum(lv[2:3] * lv[3:4], axis=-1, keepdims=True)) + lam_init)
    def scores(r):
        q = q_ref[sub * r:sub * (r + 1), :]
        lane = lax.broadcasted_iota(jnp.int32, q.shape, 1)
        zero = jnp.zeros_like(q)
        q0 = jnp.where(lane < DA_QK_DIM, q, zero)
        q1 = jnp.where(lane >= DA_QK_DIM, q, zero)
        return [_nt_dot(q0, k[...]) for k in k_refs], [_nt_dot(q1, k[...]) for k in k_refs]

    def finish(r, s0, s1):
        e0, r0 = _softmax_pieces(s0)
        e1, r1 = _softmax_pieces(s1)
        c = lam * r1 / r0
        o = r0 * functools.reduce(jnp.add, [
            jnp.dot((a - c * b).astype(BF16), v[...], preferred_element_type=F32)
            for a, b, v in zip(e0, e1, v_refs)])
        o_ref[sub * r:sub * (r + 1), :] = (_rms(o, g_ref[...]) * (1.0 - lam_init)).astype(BF16)

    _skewed(q_ref.shape[0] // sub, scores, finish)


def _skewed(n, first, second):
    pending = first(0)
    for r in range(n):
        ahead = first(r + 1) if r + 1 < n else None
        second(r, *pending)
        pending = ahead


def _mla_attn_kernel(q_ref, *refs, n_pieces, sub):
    kn_refs, kr_refs = refs[:n_pieces], refs[n_pieces:2 * n_pieces]
    vm_refs, o_ref = refs[2 * n_pieces:3 * n_pieces], refs[3 * n_pieces]

    k_cat = [jnp.concatenate([kn[...], kr[...]], axis=1) for kn, kr in zip(kn_refs, kr_refs)]

    def scores(r):
        q = q_ref[sub * r:sub * (r + 1), :]
        return ([_nt_dot(q, k) for k in k_cat],)

    def finish(r, s):
        es, rs = _softmax_pieces(s)
        o = rs * functools.reduce(jnp.add, [
            jnp.dot(e.astype(BF16), v[...], preferred_element_type=F32) for e, v in zip(es, vm_refs)])
        o_ref[sub * r:sub * (r + 1), :] = o.astype(BF16)

    _skewed(q_ref.shape[0] // sub, scores, finish)


def _outproj_kernel(x_ref, gt_ref, a_ref, b_ref, c_ref, w_ref, o_ref):
    r1, r2 = DA_WIDTH, DA_WIDTH + GM_WIDTH
    acc = (jnp.dot(a_ref[...], w_ref[:r1, :], preferred_element_type=F32)
           + jnp.dot(b_ref[...], w_ref[r1:r2, :], preferred_element_type=F32)
           + jnp.dot(c_ref[...], w_ref[r2:, :], preferred_element_type=F32))
    o_ref[...] = x_ref[...] + gt_ref[0] * acc


def _mlp_kernel(x_ref, g_ref, sh_ref, sc_ref, gt_ref, gf_ref, w1_ref, w2_ref, o_ref, h_ref, *, final_norm):
    f = pl.program_id(1)

    @pl.when(f == 0)
    def _():
        _norm_mod_rows(x_ref, g_ref, sh_ref, sc_ref, h_ref)
        o_ref[...] = jnp.zeros_like(o_ref)

    a = jnp.maximum(jnp.dot(h_ref[...], w1_ref[...].astype(BF16), preferred_element_type=F32), 0.0)
    o_ref[...] += jnp.dot((a * a).astype(BF16), w2_ref[...].astype(BF16), preferred_element_type=F32)

    @pl.when(f == pl.num_programs(1) - 1)
    def _():
        gate, gfin = gt_ref[0], gf_ref[...]

        def body(i, carry):
            chunks = _chunk_rows(i)
            outs = [x_ref[rows, :] + gate * o_ref[rows, :] for rows in chunks]
            for rows, out in zip(chunks, outs):
                o_ref[rows, :] = _rms(out, gfin) if final_norm else out
            return carry

        lax.fori_loop(0, o_ref.shape[0] // (NORM_ROWS * NORM_UNROLL), body, 0)


LATENT_ROWS = (0, SEQ)
CONTEXT_ROWS = (CTX_MOD_ROW, None)


def _mod_spec(mod_rows, tm, chunk):
    first, rows_per = mod_rows
    if rows_per is None:
        return pl.BlockSpec((1, 1, D_MODEL), lambda i, *_: (first, 0, chunk))
    return pl.BlockSpec((1, 1, D_MODEL), lambda i, *_: (first + (i * tm) // rows_per, 0, chunk))


def _mod_call(cvec, w_mod, b_mod):
    tn = 1024
    return pl.pallas_call(
        _mod_kernel,
        grid=(DEPTH, 6 * D_MODEL // tn),
        in_specs=[pl.BlockSpec((MOD_ROWS, D_MODEL), lambda l, n: (0, 0)),
                  pl.BlockSpec((1, D_MODEL, tn), lambda l, n: (l, 0, n)),
                  pl.BlockSpec((1, 1, tn), lambda l, n: (l, 0, n))],
        out_specs=pl.BlockSpec((1, MOD_ROWS, tn), lambda l, n: (l, 0, n)),
        out_shape=jax.ShapeDtypeStruct((DEPTH, MOD_ROWS, 6 * D_MODEL), F32),
        compiler_params=_cparams(("arbitrary", "arbitrary")),
        name="mod",
    )(cvec, w_mod, b_mod.reshape(DEPTH, 1, 6 * D_MODEL))


def _inproj_call(x, g, mod, mod_rows, w, layer):
    m = x.shape[0]
    tm, tn = 1024, 768
    return pl.pallas_call(
        _inproj_kernel,
        grid=(m // tm, Z_PAD // tn),
        in_specs=[pl.BlockSpec((tm, D_MODEL), lambda i, n: (i, 0)),
                  pl.BlockSpec((1, D_MODEL), lambda i, n: (0, 0)),
                  _mod_spec(mod_rows, tm, 0), _mod_spec(mod_rows, tm, 1),
                  pl.BlockSpec((None, D_MODEL, tn), lambda i, n: (layer, 0, n))],
        out_specs=pl.BlockSpec((tm, tn), lambda i, n: (i, n)),
        out_shape=jax.ShapeDtypeStruct((m, Z_PAD), F32),
        scratch_shapes=[pltpu.VMEM((tm, D_MODEL), BF16)],
        compiler_params=_cparams(("parallel", "arbitrary")),
        name="inproj",
    )(x, g, mod, mod, w)


def _zpost_call(z, cos_t, sin_t, ggm, wsp, bsp, gq, wuq, gkv, wukv):
    m = z.shape[0]
    tm = 256
    pos_tiles = cos_t.shape[0] // tm
    row = lambda i: (i, 0)
    const2 = lambda i: (0, 0)
    const3 = lambda i: (0, 0, 0)
    widths = (DA_QK_COLS, DA_QK_COLS, DA_WIDTH, GM_WIDTH, MLA_HEADS * MLA_QH,
              MLA_HEADS * MLA_NOPE, MLA_WIDTH, LANE)
    return pl.pallas_call(
        _zpost_kernel,
        grid=(m // tm,),
        in_specs=[pl.BlockSpec((tm, Z_COLS), row),
                  pl.BlockSpec((tm, LANE), lambda i: (i % pos_tiles, 0)),
                  pl.BlockSpec((tm, LANE), lambda i: (i % pos_tiles, 0)),
                  pl.BlockSpec((1, GM_WIDTH), const2),
                  pl.BlockSpec((GM_GROUPS, GM_CHUNK, GM_CHUNK), const3),
                  pl.BlockSpec((GM_GROUPS, GM_CHUNK, GM_CH), const3),
                  pl.BlockSpec((1, MLA_Q_RANK), const2),
                  pl.BlockSpec((MLA_Q_RANK, MLA_HEADS * MLA_QH), const2),
                  pl.BlockSpec((1, MLA_KV_RANK), const2),
                  pl.BlockSpec((MLA_KV_RANK, MLA_HEADS * (MLA_NOPE + MLA_V)), const2)],
        out_specs=[pl.BlockSpec((tm, w), row) for w in widths],
        out_shape=[jax.ShapeDtypeStruct((m, w), BF16) for w in widths],
        compiler_params=_cparams(("parallel",)),
        name="zpost",
    )(z, cos_t, sin_t, ggm, wsp, bsp, gq, wuq, gkv, wukv)


def _kv_specs(arrays, width):
    return [pl.BlockSpec((a.shape[0] // BATCH, width), lambda b, h, t: (b, h)) for a in arrays]


def _da_attn_call(lamv, g, q, ks, vs, lam_init):
    nq = q.shape[0] // BATCH
    tq = min(ATTN_TQ, nq)
    nqt = nq // tq
    qspec = pl.BlockSpec((tq, LANE), lambda b, h, t: (b * nqt + t, h))
    return pl.pallas_call(
        functools.partial(_da_attn_kernel, n_pieces=len(ks), lam_init=lam_init, sub=min(ATTN_SUB, tq)),
        grid=(BATCH, DA_HEADS, nqt),
        in_specs=[pl.BlockSpec(lamv.shape, lambda b, h, t: (0, 0)),
                  pl.BlockSpec((1, DA_V_DIM), lambda b, h, t: (0, 0)),
                  qspec] + _kv_specs(ks, LANE) + _kv_specs(vs, LANE),
        out_specs=qspec,
        out_shape=jax.ShapeDtypeStruct((q.shape[0], DA_WIDTH), BF16),
        compiler_params=_cparams(("parallel", "parallel", "arbitrary")),
        name="da_attn",
    )(lamv, g, q, *ks, *vs)


def _mla_attn_call(qm, kns, krs, vms):
    nq = qm.shape[0] // BATCH
    tq = min(ATTN_TQ, nq)
    nqt = nq // tq
    kr_specs = [pl.BlockSpec((a.shape[0] // BATCH, LANE), lambda b, h, t: (b, 0)) for a in krs]
    return pl.pallas_call(
        functools.partial(_mla_attn_kernel, n_pieces=len(kns), sub=min(ATTN_SUB, tq)),
        grid=(BATCH, MLA_HEADS, nqt),
        in_specs=[pl.BlockSpec((tq, MLA_QH), lambda b, h, t: (b * nqt + t, h))]
        + _kv_specs(kns, LANE) + kr_specs + _kv_specs(vms, LANE),
        out_specs=pl.BlockSpec((tq, MLA_V), lambda b, h, t: (b * nqt + t, h)),
        out_shape=jax.ShapeDtypeStruct((qm.shape[0], MLA_WIDTH), BF16),
        compiler_params=_cparams(("parallel", "parallel", "arbitrary")),
        name="mla_attn",
    )(qm, *kns, *krs, *vms)


def _outproj_call(x, mod, mod_rows, o_da, gated, o_mla, w, layer):
    m = x.shape[0]
    tm = 256
    row = lambda i: (i, 0)
    return pl.pallas_call(
        _outproj_kernel,
        grid=(m // tm,),
        in_specs=[pl.BlockSpec((tm, D_MODEL), row), _mod_spec(mod_rows, tm, 2),
                  pl.BlockSpec((tm, DA_WIDTH), row), pl.BlockSpec((tm, GM_WIDTH), row),
                  pl.BlockSpec((tm, MLA_WIDTH), row),
                  pl.BlockSpec((None, MIX_WIDTH, D_MODEL), lambda i: (layer, 0, 0))],
        out_specs=pl.BlockSpec((tm, D_MODEL), row),
        out_shape=jax.ShapeDtypeStruct((m, D_MODEL), F32),
        compiler_params=_cparams(("parallel",)),
        name="outproj",
    )(x, mod, o_da, gated, o_mla, w)


def _mlp_call(x, g, mod, mod_rows, g_final, w1, w2, layer, final_norm):
    m = x.shape[0]
    tm, tf = 1024, 512
    row = lambda i, f: (i, 0)
    return pl.pallas_call(
        functools.partial(_mlp_kernel, final_norm=final_norm),
        grid=(m // tm, D_FF // tf),
        in_specs=[pl.BlockSpec((tm, D_MODEL), row, pipeline_mode=pl.Buffered(1)),
                  pl.BlockSpec((1, D_MODEL), lambda i, f: (0, 0)),
                  _mod_spec(mod_rows, tm, 3), _mod_spec(mod_rows, tm, 4), _mod_spec(mod_rows, tm, 5),
                  pl.BlockSpec((1, D_MODEL), lambda i, f: (0, 0)),
                  pl.BlockSpec((None, D_MODEL, tf), lambda i, f: (layer, 0, f)),
                  pl.BlockSpec((None, tf, D_MODEL), lambda i, f: (layer, f, 0))],
        out_specs=pl.BlockSpec((tm, D_MODEL), row),
        out_shape=jax.ShapeDtypeStruct((m, D_MODEL), F32),
        scratch_shapes=[pltpu.VMEM((tm, D_MODEL), BF16)],
        compiler_params=_cparams(("parallel", "arbitrary")),
        name="mlp",
    )(x, g, mod, mod, mod, g_final, w1, w2)


def kernel(x, c, ctx, c_ctx, w_mod, b_mod, g_norm_mix, g_norm_mlp, w_in, lam_q1, lam_k1, lam_q2, lam_k2,
           g_da_sub, g_gm_v, w_spatial, b_spatial, g_mla_q, w_mla_uq, g_mla_kv, w_mla_ukv, w_out, w_fc1,
           w_fc2, g_final):
    cos_t, sin_t = _rope_tables()
    cos_c = jnp.ones((CTX_LEN, LANE), F32)
    sin_c = jnp.zeros((CTX_LEN, LANE), F32)

    xl = x.reshape(BATCH * SEQ, D_MODEL)
    xc = ctx.reshape(BATCH * CTX_LEN, D_MODEL)
    cvec = jnp.zeros((MOD_ROWS, D_MODEL), F32).at[:BATCH].set(c).at[CTX_MOD_ROW].set(c_ctx)
    mod_all = _mod_call(cvec, w_mod, b_mod)

    g_final2 = g_final.reshape(1, D_MODEL)
    w_in_p = _w_in_layout(w_in)
    w_out_b = w_out.astype(BF16)

    for l in range(DEPTH):
        last = l == DEPTH - 1
        lam_init = 0.8 - 0.6 * math.exp(-0.3 * l)
        mod = mod_all[l].reshape(MOD_ROWS, 1, 6 * D_MODEL)
        wuq_p = _w_uq_layout(w_mla_uq[l])
        wukv_p = w_mla_ukv[l].astype(BF16)
        g_mix, g_mlp = g_norm_mix[l].reshape(1, D_MODEL), g_norm_mlp[l].reshape(1, D_MODEL)
        lamv = jnp.stack([lam_q1[l], lam_k1[l], lam_q2[l], lam_k2[l]])
        g_sub = g_da_sub[l].reshape(1, DA_V_DIM)
        post_w = (g_gm_v[l].reshape(1, GM_WIDTH), w_spatial[l].astype(BF16),
                  jnp.broadcast_to(b_spatial[l][:, :, None], (GM_GROUPS, GM_CHUNK, GM_CH)),
                  g_mla_q[l].reshape(1, MLA_Q_RANK), wuq_p, g_mla_kv[l].reshape(1, MLA_KV_RANK), wukv_p)

        z_l = _inproj_call(xl, g_mix, mod, LATENT_ROWS, w_in_p, l)
        z_c = _inproj_call(xc, g_mix, mod, CONTEXT_ROWS, w_in_p, l)
        q, k, v, gated, qm, kn, vm, kr = _zpost_call(z_l, cos_t, sin_t, *post_w)
        qc, kc, vc, gated_c, qmc, knc, vmc, krc = _zpost_call(z_c, cos_c, sin_c, *post_w)

        o_da = _da_attn_call(lamv, g_sub, q, [k, kc], [v, vc], lam_init)
        o_mla = _mla_attn_call(qm, [kn, knc], [kr, krc], [vm, vmc])
        xl = _outproj_call(xl, mod, LATENT_ROWS, o_da, gated, o_mla, w_out_b, l)
        xl = _mlp_call(xl, g_mlp, mod, LATENT_ROWS, g_final2, w_fc1, w_fc2, l, final_norm=last)

        if not last:
            o_da_c = _da_attn_call(lamv, g_sub, qc, [kc], [vc], lam_init)
            o_mla_c = _mla_attn_call(qmc, [knc], [krc], [vmc])
            xc = _outproj_call(xc, mod, CONTEXT_ROWS, o_da_c, gated_c, o_mla_c, w_out_b, l)
            xc = _mlp_call(xc, g_mlp, mod, CONTEXT_ROWS, g_final2, w_fc1, w_fc2, l, final_norm=False)

    return xl.reshape(BATCH, SEQ, D_MODEL)
```

```python
import functools
import math

import numpy as np
import jax
import jax.numpy as jnp
from jax import lax
from jax.experimental import pallas as pl
from jax.experimental.pallas import tpu as pltpu

D_MODEL = 2048
BATCH = 4
SEQ = 2048
DEPTH = 2
CTX_LEN = 256
GRID_W = 64
EPS = 1e-6
ROPE_THETA = 10000.0

DA_HEADS = 6
DA_QK_DIM = 64
DA_V_DIM = 128
DA_WIDTH = DA_HEADS * DA_V_DIM
DA_QK_COLS = DA_HEADS * 2 * DA_QK_DIM
DA_SCALE = DA_QK_DIM ** -0.5

GM_GROUPS = 4
GM_CH = 128
GM_CHUNK = 128
GM_WIDTH = GM_GROUPS * GM_CH

MLA_HEADS = 6
MLA_Q_RANK = 512
MLA_KV_RANK = 512
MLA_NOPE = 128
MLA_ROPE = 64
MLA_V = 128
MLA_WIDTH = MLA_HEADS * MLA_V
MIX_WIDTH = DA_WIDTH + GM_WIDTH + MLA_WIDTH
MLA_SCALE = (MLA_NOPE + MLA_ROPE) ** -0.5
LOG2E = math.log2(math.e)

ROT_DIM = 64
D_FF = 4 * D_MODEL
MOD_ROWS = 8
CTX_MOD_ROW = BATCH

LANE = 128
Z_Q, Z_K, Z_V, Z_GU, Z_GV, Z_CQ, Z_CKV, Z_KR = 0, 768, 1536, 2304, 2816, 3328, 3840, 4352
Z_COLS = Z_KR + LANE
Z_PAD = 4608
NORM_ROWS = 16
NORM_UNROLL = 4
MLA_QH = 2 * LANE

VMEM_LIMIT = 56 * 1024 * 1024
ATTN_TQ = 2048
ATTN_SUB = 128
MLA_TQ = 2048
MLA_SUB = 256

BF16 = jnp.bfloat16
F32 = jnp.float32


def _cparams(sem):
    return pltpu.CompilerParams(dimension_semantics=sem, vmem_limit_bytes=VMEM_LIMIT)


def _w_in_layout(w):
    return jnp.pad(w, ((0, 0), (0, 0), (0, Z_PAD - w.shape[-1]))).astype(BF16)


def _w_uq_layout(w):
    k = w.shape[0]
    w = w.reshape(k, MLA_HEADS, MLA_NOPE + MLA_ROPE)
    return jnp.pad(w, ((0, 0), (0, 0), (0, MLA_QH - MLA_NOPE - MLA_ROPE))).reshape(k, -1).astype(BF16)


def _rope_tables():
    rows = SEQ // GRID_W
    row = jnp.repeat(jnp.arange(rows, dtype=F32), GRID_W)
    col = jnp.tile(jnp.arange(GRID_W, dtype=F32), rows)
    n_f = ROT_DIM // 4
    inv = ROPE_THETA ** (-jnp.arange(n_f, dtype=F32) / n_f)
    ang = jnp.concatenate([row[:, None] * inv, col[:, None] * inv], axis=-1)
    cos, sin = jnp.cos(ang), jnp.sin(ang)
    cos_u = jnp.repeat(cos, 2, axis=-1)
    sin_u = jnp.stack([-sin, sin], axis=-1).reshape(SEQ, ROT_DIM)
    return jnp.tile(cos_u, (1, 2)), jnp.tile(sin_u, (1, 2))


def _rms(x, g):
    return x * lax.rsqrt(jnp.mean(x * x, axis=-1, keepdims=True) + EPS) * g


def _gelu(x):
    return 0.5 * x * (1.0 + lax.erf(x * math.sqrt(0.5)))


def _rope128(x, cos, sin):
    lane = lax.broadcasted_iota(jnp.int32, x.shape, 1)
    partner = jnp.where((lane & 1) == 0, pltpu.roll(x, LANE - 1, 1), pltpu.roll(x, 1, 1))
    return x * cos + partner * sin


def _norm_mod_rows(x_ref, g_ref, sh_ref, sc_ref, h_ref):
    gain = g_ref[...] * (1.0 + sc_ref[0])
    shift = sh_ref[0]

    def body(i, carry):
        for rows in _chunk_rows(i):
            x = x_ref[rows, :]
            inv = lax.rsqrt(jnp.mean(x * x, axis=-1, keepdims=True) + EPS)
            h_ref[rows, :] = (x * inv * gain + shift).astype(BF16)
        return carry

    lax.fori_loop(0, x_ref.shape[0] // (NORM_ROWS * NORM_UNROLL), body, 0)


def _chunk_rows(i):
    base = pl.multiple_of(i * (NORM_ROWS * NORM_UNROLL), NORM_ROWS * NORM_UNROLL)
    return [pl.ds(base + NORM_ROWS * k, NORM_ROWS) for k in range(NORM_UNROLL)]


def _mod_kernel(c_ref, w_ref, b_ref, o_ref):
    c = c_ref[...]
    s = (c / (1.0 + jnp.exp(-c))).astype(BF16)
    o_ref[0] = jnp.dot(s, w_ref[0].astype(BF16), preferred_element_type=F32) + b_ref[0]


def _inproj_kernel(x_ref, g_ref, sh_ref, sc_ref, w_ref, z_ref, h_ref):
    @pl.when(pl.program_id(1) == 0)
    def _():
        _norm_mod_rows(x_ref, g_ref, sh_ref, sc_ref, h_ref)

    z_ref[...] = jnp.dot(h_ref[...], w_ref[...], preferred_element_type=F32)


def _zpost_kernel(z_ref, cos_ref, sin_ref, ggm_ref, wsp_ref, bsp_ref, gq_ref, wuq_ref, gkv_ref, wukv_ref,
                  q_ref, k_ref, v_ref, gated_ref, qm_ref, kn_ref, vm_ref):
    tm = z_ref.shape[0]
    cos, sin = cos_ref[...], sin_ref[...]
    for j in range(DA_HEADS):
        sl = slice(LANE * j, LANE * (j + 1))
        q_ref[:, sl] = (_rope128(z_ref[:, Z_Q + LANE * j:Z_Q + LANE * (j + 1)], cos, sin)
                        * (DA_SCALE * LOG2E)).astype(BF16)
        k_ref[:, sl] = _rope128(z_ref[:, Z_K + LANE * j:Z_K + LANE * (j + 1)], cos, sin).astype(BF16)
    v_ref[...] = z_ref[:, Z_V:Z_GU].astype(BF16)
    kr = _rope128(z_ref[:, Z_KR:Z_COLS], cos, sin).astype(BF16)

    for g in range(GM_GROUPS):
        sl = slice(GM_CH * g, GM_CH * (g + 1))
        u = _gelu(z_ref[:, Z_GU + GM_CH * g:Z_GU + GM_CH * (g + 1)])
        v = _rms(_gelu(z_ref[:, Z_GV + GM_CH * g:Z_GV + GM_CH * (g + 1)]), ggm_ref[:, sl]).astype(BF16)
        for c in range(tm // GM_CHUNK):
            rows = slice(GM_CHUNK * c, GM_CHUNK * (c + 1))
            mixed = jnp.dot(wsp_ref[g], v[rows], preferred_element_type=F32) + bsp_ref[g]
            gated_ref[rows, sl] = (u[rows] * mixed).astype(BF16)

    cq = _rms(z_ref[:, Z_CQ:Z_CKV], gq_ref[...]).astype(BF16)
    qm = jnp.dot(cq, wuq_ref[...], preferred_element_type=F32)
    for h in range(MLA_HEADS):
        lo = MLA_QH * h
        qm_ref[:, lo:lo + LANE] = (qm[:, lo:lo + LANE] * (MLA_SCALE * LOG2E)).astype(BF16)
        qm_ref[:, lo + LANE:lo + MLA_QH] = (
            _rope128(qm[:, lo + LANE:lo + MLA_QH], cos, sin) * (MLA_SCALE * LOG2E)).astype(BF16)

    ckv = _rms(z_ref[:, Z_CKV:Z_KR], gkv_ref[...]).astype(BF16)
    kv = jnp.dot(ckv, wukv_ref[...], preferred_element_type=F32)
    for h in range(MLA_HEADS):
        lo = (MLA_NOPE + MLA_V) * h
        kn_ref[:, MLA_QH * h:MLA_QH * h + LANE] = kv[:, lo:lo + MLA_NOPE].astype(BF16)
        kn_ref[:, MLA_QH * h + LANE:MLA_QH * (h + 1)] = kr
        vm_ref[:, MLA_V * h:MLA_V * (h + 1)] = kv[:, lo + MLA_NOPE:lo + MLA_NOPE + MLA_V].astype(BF16)


def _nt_dot(a, b):
    return lax.dot_general(a, b, (((1,), (1,)), ((), ())), preferred_element_type=F32)


def _softmax_pieces(scores):
    m = functools.reduce(jnp.maximum, [jnp.max(s, axis=-1, keepdims=True) for s in scores])
    es = [jnp.exp2(s - m) for s in scores]
    total = functools.reduce(jnp.add, [jnp.sum(e, axis=-1, keepdims=True) for e in es])
    return es, 1.0 / total


def _da_attn_kernel(lam_ref, g_ref, q_ref, *refs, n_pieces, lam_init, sub):
    k_refs, v_refs, o_ref = refs[:n_pieces], refs[n_pieces:2 * n_pieces], refs[2 * n_pieces]
    lv = lam_ref[...]
    lam = (jnp.exp(jnp.sum(lv[0:1] * lv[1:2], axis=-1, keepdims=True))
           - jnp.exp(jnp.sum(lv[2:3] * lv[3:4], axis=-1, keepdims=True)) + lam_init)
    def scores(r):
        q = q_ref[sub * r:sub * (r + 1), :]
        lane = lax.broadcasted_iota(jnp.int32, q.shape, 1)
        zero = jnp.zeros_like(q)
        q0 = jnp.where(lane < DA_QK_DIM, q, zero)
        q1 = jnp.where(lane >= DA_QK_DIM, q, zero)
        return [_nt_dot(q0, k[...]) for k in k_refs], [_nt_dot(q1, k[...]) for k in k_refs]

    def finish(r, s0, s1):
        e0, r0 = _softmax_pieces(s0)
        e1, r1 = _softmax_pieces(s1)
        c = lam * r1 / r0
        o = r0 * functools.reduce(jnp.add, [
            jnp.dot((a - c * b).astype(BF16), v[...], preferred_element_type=F32)
            for a, b, v in zip(e0, e1, v_refs)])
        o_ref[sub * r:sub * (r + 1), :] = (_rms(o, g_ref[...]) * (1.0 - lam_init)).astype(BF16)

    _skewed(q_ref.shape[0] // sub, scores, finish)


def _skewed(n, first, second):
    pending = first(0)
    for r in range(n):
        ahead = first(r + 1) if r + 1 < n else None
        second(r, *pending)
        pending = ahead


def _mla_attn_kernel(q_ref, *refs, n_pieces, sub):
    k_refs, vm_refs, o_ref = refs[:n_pieces], refs[n_pieces:2 * n_pieces], refs[2 * n_pieces]

    def scores(r):
        q = q_ref[sub * r:sub * (r + 1), :]
        return ([_nt_dot(q, k[...]) for k in k_refs],)

    def finish(r, s):
        es, rs = _softmax_pieces(s)
        o = rs * functools.reduce(jnp.add, [
            jnp.dot(e.astype(BF16), v[...], preferred_element_type=F32) for e, v in zip(es, vm_refs)])
        o_ref[sub * r:sub * (r + 1), :] = o.astype(BF16)

    _skewed(q_ref.shape[0] // sub, scores, finish)


def _outproj_kernel(x_ref, gt_ref, a_ref, b_ref, c_ref, w_ref, o_ref):
    r1, r2 = DA_WIDTH, DA_WIDTH + GM_WIDTH
    acc = (jnp.dot(a_ref[...], w_ref[:r1, :], preferred_element_type=F32)
           + jnp.dot(b_ref[...], w_ref[r1:r2, :], preferred_element_type=F32)
           + jnp.dot(c_ref[...], w_ref[r2:, :], preferred_element_type=F32))
    o_ref[...] = x_ref[...] + gt_ref[0] * acc


def _mlp_kernel(x_ref, g_ref, sh_ref, sc_ref, gt_ref, gf_ref, w1_ref, w2_ref, o_ref, h_ref, *, final_norm):
    f = pl.program_id(1)

    @pl.when(f == 0)
    def _():
        _norm_mod_rows(x_ref, g_ref, sh_ref, sc_ref, h_ref)
        o_ref[...] = jnp.zeros_like(o_ref)

    a = jnp.maximum(jnp.dot(h_ref[...], w1_ref[...].astype(BF16), preferred_element_type=F32), 0.0)
    o_ref[...] += jnp.dot((a * a).astype(BF16), w2_ref[...].astype(BF16), preferred_element_type=F32)

    @pl.when(f == pl.num_programs(1) - 1)
    def _():
        gate, gfin = gt_ref[0], gf_ref[...]

        def body(i, carry):
            chunks = _chunk_rows(i)
            outs = [x_ref[rows, :] + gate * o_ref[rows, :] for rows in chunks]
            for rows, out in zip(chunks, outs):
                o_ref[rows, :] = _rms(out, gfin) if final_norm else out
            return carry

        lax.fori_loop(0, o_ref.shape[0] // (NORM_ROWS * NORM_UNROLL), body, 0)


LATENT_ROWS = (0, SEQ)
CONTEXT_ROWS = (CTX_MOD_ROW, None)


def _mod_spec(mod_rows, tm, chunk):
    first, rows_per = mod_rows
    if rows_per is None:
        return pl.BlockSpec((1, 1, D_MODEL), lambda i, *_: (first, 0, chunk))
    return pl.BlockSpec((1, 1, D_MODEL), lambda i, *_: (first + (i * tm) // rows_per, 0, chunk))


def _mod_call(cvec, w_mod, b_mod):
    tn = 1024
    return pl.pallas_call(
        _mod_kernel,
        grid=(DEPTH, 6 * D_MODEL // tn),
        in_specs=[pl.BlockSpec((MOD_ROWS, D_MODEL), lambda l, n: (0, 0)),
                  pl.BlockSpec((1, D_MODEL, tn), lambda l, n: (l, 0, n)),
                  pl.BlockSpec((1, 1, tn), lambda l, n: (l, 0, n))],
        out_specs=pl.BlockSpec((1, MOD_ROWS, tn), lambda l, n: (l, 0, n)),
        out_shape=jax.ShapeDtypeStruct((DEPTH, MOD_ROWS, 6 * D_MODEL), F32),
        compiler_params=_cparams(("arbitrary", "arbitrary")),
        name="mod",
    )(cvec, w_mod, b_mod.reshape(DEPTH, 1, 6 * D_MODEL))


def _inproj_call(x, g, mod, mod_rows, w, layer):
    m = x.shape[0]
    tm, tn = 1024, 768
    return pl.pallas_call(
        _inproj_kernel,
        grid=(m // tm, Z_PAD // tn),
        in_specs=[pl.BlockSpec((tm, D_MODEL), lambda i, n: (i, 0)),
                  pl.BlockSpec((1, D_MODEL), lambda i, n: (0, 0)),
                  _mod_spec(mod_rows, tm, 0), _mod_spec(mod_rows, tm, 1),
                  pl.BlockSpec((None, D_MODEL, tn), lambda i, n: (layer, 0, n))],
        out_specs=pl.BlockSpec((tm, tn), lambda i, n: (i, n)),
        out_shape=jax.ShapeDtypeStruct((m, Z_PAD), F32),
        scratch_shapes=[pltpu.VMEM((tm, D_MODEL), BF16)],
        compiler_params=_cparams(("parallel", "arbitrary")),
        name="inproj",
    )(x, g, mod, mod, w)


def _zpost_call(z, cos_t, sin_t, ggm, wsp, bsp, gq, wuq, gkv, wukv):
    m = z.shape[0]
    tm = 256
    pos_tiles = cos_t.shape[0] // tm
    row = lambda i: (i, 0)
    const2 = lambda i: (0, 0)
    const3 = lambda i: (0, 0, 0)
    widths = (DA_QK_COLS, DA_QK_COLS, DA_WIDTH, GM_WIDTH, MLA_HEADS * MLA_QH,
              MLA_HEADS * MLA_QH, MLA_WIDTH)
    return pl.pallas_call(
        _zpost_kernel,
        grid=(m // tm,),
        in_specs=[pl.BlockSpec((tm, Z_COLS), row),
                  pl.BlockSpec((tm, LANE), lambda i: (i % pos_tiles, 0)),
                  pl.BlockSpec((tm, LANE), lambda i: (i % pos_tiles, 0)),
                  pl.BlockSpec((1, GM_WIDTH), const2),
                  pl.BlockSpec((GM_GROUPS, GM_CHUNK, GM_CHUNK), const3),
                  pl.BlockSpec((GM_GROUPS, GM_CHUNK, GM_CH), const3),
                  pl.BlockSpec((1, MLA_Q_RANK), const2),
                  pl.BlockSpec((MLA_Q_RANK, MLA_HEADS * MLA_QH), const2),
                  pl.BlockSpec((1, MLA_KV_RANK), const2),
                  pl.BlockSpec((MLA_KV_RANK, MLA_HEADS * (MLA_NOPE + MLA_V)), const2)],
        out_specs=[pl.BlockSpec((tm, w), row) for w in widths],
        out_shape=[jax.ShapeDtypeStruct((m, w), BF16) for w in widths],
        compiler_params=_cparams(("parallel",)),
        name="zpost",
    )(z, cos_t, sin_t, ggm, wsp, bsp, gq, wuq, gkv, wukv)


def _kv_specs(arrays, width):
    return [pl.BlockSpec((a.shape[0] // BATCH, width), lambda b, h, t: (b, h)) for a in arrays]


def _da_attn_call(lamv, g, q, ks, vs, lam_init):
    nq = q.shape[0] // BATCH
    tq = min(ATTN_TQ, nq)
    nqt = nq // tq
    qspec = pl.BlockSpec((tq, LANE), lambda b, h, t: (b * nqt + t, h))
    return pl.pallas_call(
        functools.partial(_da_attn_kernel, n_pieces=len(ks), lam_init=lam_init, sub=min(ATTN_SUB, tq)),
        grid=(BATCH, DA_HEADS, nqt),
        in_specs=[pl.BlockSpec(lamv.shape, lambda b, h, t: (0, 0)),
                  pl.BlockSpec((1, DA_V_DIM), lambda b, h, t: (0, 0)),
                  qspec] + _kv_specs(ks, LANE) + _kv_specs(vs, LANE),
        out_specs=qspec,
        out_shape=jax.ShapeDtypeStruct((q.shape[0], DA_WIDTH), BF16),
        compiler_params=_cparams(("parallel", "parallel", "arbitrary")),
        name="da_attn",
    )(lamv, g, q, *ks, *vs)


def _mla_attn_call(qm, ks, vms):
    nq = qm.shape[0] // BATCH
    tq = min(MLA_TQ, nq)
    nqt = nq // tq
    return pl.pallas_call(
        functools.partial(_mla_attn_kernel, n_pieces=len(ks), sub=min(MLA_SUB, tq)),
        grid=(BATCH, MLA_HEADS, nqt),
        in_specs=[pl.BlockSpec((tq, MLA_QH), lambda b, h, t: (b * nqt + t, h))]
        + _kv_specs(ks, MLA_QH) + _kv_specs(vms, LANE),
        out_specs=pl.BlockSpec((tq, MLA_V), lambda b, h, t: (b * nqt + t, h)),
        out_shape=jax.ShapeDtypeStruct((qm.shape[0], MLA_WIDTH), BF16),
        compiler_params=_cparams(("parallel", "parallel", "arbitrary")),
        name="mla_attn",
    )(qm, *ks, *vms)


def _outproj_call(x, mod, mod_rows, o_da, gated, o_mla, w, layer):
    m = x.shape[0]
    tm = 256
    row = lambda i: (i, 0)
    return pl.pallas_call(
        _outproj_kernel,
        grid=(m // tm,),
        in_specs=[pl.BlockSpec((tm, D_MODEL), row), _mod_spec(mod_rows, tm, 2),
                  pl.BlockSpec((tm, DA_WIDTH), row), pl.BlockSpec((tm, GM_WIDTH), row),
                  pl.BlockSpec((tm, MLA_WIDTH), row),
                  pl.BlockSpec((None, MIX_WIDTH, D_MODEL), lambda i: (layer, 0, 0))],
        out_specs=pl.BlockSpec((tm, D_MODEL), row),
        out_shape=jax.ShapeDtypeStruct((m, D_MODEL), F32),
        compiler_params=_cparams(("parallel",)),
        name="outproj",
    )(x, mod, o_da, gated, o_mla, w)


def _mlp_call(x, g, mod, mod_rows, g_final, w1, w2, layer, final_norm):
    m = x.shape[0]
    tm, tf = 1024, 512
    row = lambda i, f: (i, 0)
    return pl.pallas_call(
        functools.partial(_mlp_kernel, final_norm=final_norm),
        grid=(m // tm, D_FF // tf),
        in_specs=[pl.BlockSpec((tm, D_MODEL), row, pipeline_mode=pl.Buffered(1)),
                  pl.BlockSpec((1, D_MODEL), lambda i, f: (0, 0)),
                  _mod_spec(mod_rows, tm, 3), _mod_spec(mod_rows, tm, 4), _mod_spec(mod_rows, tm, 5),
                  pl.BlockSpec((1, D_MODEL), lambda i, f: (0, 0)),
                  pl.BlockSpec((None, D_MODEL, tf), lambda i, f: (layer, 0, f)),
                  pl.BlockSpec((None, tf, D_MODEL), lambda i, f: (layer, f, 0))],
        out_specs=pl.BlockSpec((tm, D_MODEL), row),
        out_shape=jax.ShapeDtypeStruct((m, D_MODEL), F32),
        scratch_shapes=[pltpu.VMEM((tm, D_MODEL), BF16)],
        compiler_params=_cparams(("parallel", "arbitrary")),
        name="mlp",
    )(x, g, mod, mod, mod, g_final, w1, w2)


def kernel(x, c, ctx, c_ctx, w_mod, b_mod, g_norm_mix, g_norm_mlp, w_in, lam_q1, lam_k1, lam_q2, lam_k2,
           g_da_sub, g_gm_v, w_spatial, b_spatial, g_mla_q, w_mla_uq, g_mla_kv, w_mla_ukv, w_out, w_fc1,
           w_fc2, g_final):
    cos_t, sin_t = _rope_tables()
    cos_c = jnp.ones((CTX_LEN, LANE), F32)
    sin_c = jnp.zeros((CTX_LEN, LANE), F32)

    xl = x.reshape(BATCH * SEQ, D_MODEL)
    xc = ctx.reshape(BATCH * CTX_LEN, D_MODEL)
    cvec = jnp.zeros((MOD_ROWS, D_MODEL), F32).at[:BATCH].set(c).at[CTX_MOD_ROW].set(c_ctx)
    mod_all = _mod_call(cvec, w_mod, b_mod)

    g_final2 = g_final.reshape(1, D_MODEL)
    w_in_p = _w_in_layout(w_in)
    w_out_b = w_out.astype(BF16)

    for l in range(DEPTH):
        last = l == DEPTH - 1
        lam_init = 0.8 - 0.6 * math.exp(-0.3 * l)
        mod = mod_all[l].reshape(MOD_ROWS, 1, 6 * D_MODEL)
        wuq_p = _w_uq_layout(w_mla_uq[l])
        wukv_p = w_mla_ukv[l].astype(BF16)
        g_mix, g_mlp = g_norm_mix[l].reshape(1, D_MODEL), g_norm_mlp[l].reshape(1, D_MODEL)
        lamv = jnp.stack([lam_q1[l], lam_k1[l], lam_q2[l], lam_k2[l]])
        g_sub = g_da_sub[l].reshape(1, DA_V_DIM)
        post_w = (g_gm_v[l].reshape(1, GM_WIDTH), w_spatial[l].astype(BF16),
                  jnp.broadcast_to(b_spatial[l][:, :, None], (GM_GROUPS, GM_CHUNK, GM_CH)),
                  g_mla_q[l].reshape(1, MLA_Q_RANK), wuq_p, g_mla_kv[l].reshape(1, MLA_KV_RANK), wukv_p)

        z_l = _inproj_call(xl, g_mix, mod, LATENT_ROWS, w_in_p, l)
        z_c = _inproj_call(xc, g_mix, mod, CONTEXT_ROWS, w_in_p, l)
        q, k, v, gated, qm, kn, vm = _zpost_call(z_l, cos_t, sin_t, *post_w)
        qc, kc, vc, gated_c, qmc, knc, vmc = _zpost_call(z_c, cos_c, sin_c, *post_w)

        o_da = _da_attn_call(lamv, g_sub, q, [k, kc], [v, vc], lam_init)
        o_mla = _mla_attn_call(qm, [kn, knc], [vm, vmc])
        xl = _outproj_call(xl, mod, LATENT_ROWS, o_da, gated, o_mla, w_out_b, l)
        xl = _mlp_call(xl, g_mlp, mod, LATENT_ROWS, g_final2, w_fc1, w_fc2, l, final_norm=last)

        if not last:
            o_da_c = _da_attn_call(lamv, g_sub, qc, [kc], [vc], lam_init)
            o_mla_c = _mla_attn_call(qmc, [knc], [vmc])
            xc = _outproj_call(xc, mod, CONTEXT_ROWS, o_da_c, gated_c, o_mla_c, w_out_b, l)
            xc = _mlp_call(xc, g_mlp, mod, CONTEXT_ROWS, g_final2, w_fc1, w_fc2, l, final_norm=False)

    return xl.reshape(BATCH, SEQ, D_MODEL)
```

```python
import functools
import math

import jax
import jax.numpy as jnp
from jax import lax
from jax.experimental import pallas as pl
from jax.experimental.pallas import tpu as pltpu

D_MODEL = 2048
BATCH = 4
SEQ = 2048
DEPTH = 2
CTX_LEN = 256
GRID_W = 64
EPS = 1e-6
ROPE_THETA = 10000.0

DA_HEADS = 6
DA_QK_DIM = 64
DA_V_DIM = 128
DA_WIDTH = DA_HEADS * DA_V_DIM
DA_QK_COLS = DA_HEADS * 2 * DA_QK_DIM
DA_SCALE = DA_QK_DIM ** -0.5

GM_GROUPS = 4
GM_CH = 128
GM_CHUNK = 128
GM_WIDTH = GM_GROUPS * GM_CH

MLA_HEADS = 6
MLA_Q_RANK = 512
MLA_KV_RANK = 512
MLA_NOPE = 128
MLA_ROPE = 64
MLA_V = 128
MLA_WIDTH = MLA_HEADS * MLA_V
MIX_WIDTH = DA_WIDTH + GM_WIDTH + MLA_WIDTH
MLA_SCALE = (MLA_NOPE + MLA_ROPE) ** -0.5
LOG2E = math.log2(math.e)

ROT_DIM = 64
D_FF = 4 * D_MODEL
MOD_ROWS = 8
CTX_MOD_ROW = BATCH

LANE = 128
Z_Q, Z_K, Z_V, Z_GU, Z_GV, Z_CQ, Z_CKV, Z_KR = 0, 768, 1536, 2304, 2816, 3328, 3840, 4352
Z_COLS = Z_KR + LANE
NORM_ROWS = 16
NORM_UNROLL = 4
MLA_QH = 2 * LANE

VMEM_LIMIT = 56 * 1024 * 1024
ATTN_TQ = 2048
ATTN_SUB = 128
MLA_TQ = 2048
MLA_SUB = 256

BF16 = jnp.bfloat16
F32 = jnp.float32


def _cparams(sem):
    return pltpu.CompilerParams(dimension_semantics=sem, vmem_limit_bytes=VMEM_LIMIT)


def _w_in_layout(w):
    return jnp.pad(w, ((0, 0), (0, 0), (0, Z_COLS - w.shape[-1]))).astype(BF16)


def _w_uq_layout(w):
    k = w.shape[0]
    w = w.reshape(k, MLA_HEADS, MLA_NOPE + MLA_ROPE)
    return jnp.pad(w, ((0, 0), (0, 0), (0, MLA_QH - MLA_NOPE - MLA_ROPE))).reshape(k, -1).astype(BF16)


def _rope_tables():
    rows = SEQ // GRID_W
    row = jnp.repeat(jnp.arange(rows, dtype=F32), GRID_W)
    col = jnp.tile(jnp.arange(GRID_W, dtype=F32), rows)
    n_f = ROT_DIM // 4
    inv = ROPE_THETA ** (-jnp.arange(n_f, dtype=F32) / n_f)
    ang = jnp.concatenate([row[:, None] * inv, col[:, None] * inv], axis=-1)
    cos, sin = jnp.cos(ang), jnp.sin(ang)
    cos_u = jnp.repeat(cos, 2, axis=-1)
    sin_u = jnp.stack([-sin, sin], axis=-1).reshape(SEQ, ROT_DIM)
    return jnp.tile(cos_u, (1, 2)), jnp.tile(sin_u, (1, 2))


def _rms(x, g):
    return x * lax.rsqrt(jnp.mean(x * x, axis=-1, keepdims=True) + EPS) * g


def _gelu(x):
    return 0.5 * x * (1.0 + lax.erf(x * math.sqrt(0.5)))


def _rope128(x, cos, sin):
    lane = lax.broadcasted_iota(jnp.int32, x.shape, 1)
    partner = jnp.where((lane & 1) == 0, pltpu.roll(x, LANE - 1, 1), pltpu.roll(x, 1, 1))
    return x * cos + partner * sin


def _norm_mod_rows(x_ref, g_ref, sh_ref, sc_ref, h_ref):
    gain = g_ref[...] * (1.0 + sc_ref[0])
    shift = sh_ref[0]

    def body(i, carry):
        for rows in _chunk_rows(i):
            x = x_ref[rows, :]
            inv = lax.rsqrt(jnp.mean(x * x, axis=-1, keepdims=True) + EPS)
            h_ref[rows, :] = (x * inv * gain + shift).astype(BF16)
        return carry

    lax.fori_loop(0, x_ref.shape[0] // (NORM_ROWS * NORM_UNROLL), body, 0)


def _chunk_rows(i):
    base = pl.multiple_of(i * (NORM_ROWS * NORM_UNROLL), NORM_ROWS * NORM_UNROLL)
    return [pl.ds(base + NORM_ROWS * k, NORM_ROWS) for k in range(NORM_UNROLL)]


def _mod_kernel(c_ref, w_ref, b_ref, o_ref):
    c = c_ref[...]
    s = (c / (1.0 + jnp.exp(-c))).astype(BF16)
    o_ref[0] = jnp.dot(s, w_ref[0].astype(BF16), preferred_element_type=F32) + b_ref[0]


def _mix_in_kernel(x_ref, g_ref, sh_ref, sc_ref, w_ref, cos_ref, sin_ref, ggm_ref, wsp_ref, bsp_ref,
                   gq_ref, wuq_ref, gkv_ref, wukv_ref,
                   q_ref, k_ref, v_ref, gated_ref, qm_ref, kn_ref, vm_ref, h_ref):
    _norm_mod_rows(x_ref, g_ref, sh_ref, sc_ref, h_ref)
    tm = h_ref.shape[0]
    cos, sin = cos_ref[...], sin_ref[...]

    def seg(lo, hi):
        return jnp.dot(h_ref[...], w_ref[:, lo:hi], preferred_element_type=F32)

    zq = seg(Z_Q, Z_K)
    zk = seg(Z_K, Z_V)
    for j in range(DA_HEADS):
        sl = slice(LANE * j, LANE * (j + 1))
        q_ref[:, sl] = (_rope128(zq[:, sl], cos, sin) * (DA_SCALE * LOG2E)).astype(BF16)
    zv = seg(Z_V, Z_GU)
    for j in range(DA_HEADS):
        sl = slice(LANE * j, LANE * (j + 1))
        k_ref[:, sl] = _rope128(zk[:, sl], cos, sin).astype(BF16)
    zg = seg(Z_GU, Z_CQ)
    v_ref[...] = zv.astype(BF16)
    zc = seg(Z_CQ, Z_COLS)

    for g in range(GM_GROUPS):
        sl = slice(GM_CH * g, GM_CH * (g + 1))
        u = _gelu(zg[:, GM_CH * g:GM_CH * (g + 1)])
        v = _rms(_gelu(zg[:, GM_WIDTH + GM_CH * g:GM_WIDTH + GM_CH * (g + 1)]), ggm_ref[:, sl]).astype(BF16)
        for c in range(tm // GM_CHUNK):
            rows = slice(GM_CHUNK * c, GM_CHUNK * (c + 1))
            mixed = jnp.dot(wsp_ref[g], v[rows], preferred_element_type=F32) + bsp_ref[g]
            gated_ref[rows, sl] = (u[rows] * mixed).astype(BF16)

    kr = _rope128(zc[:, Z_KR - Z_CQ:], cos, sin).astype(BF16)
    cq = _rms(zc[:, :MLA_Q_RANK], gq_ref[...]).astype(BF16)
    qm = jnp.dot(cq, wuq_ref[...], preferred_element_type=F32)
    for h in range(MLA_HEADS):
        lo = MLA_QH * h
        qm_ref[:, lo:lo + LANE] = (qm[:, lo:lo + LANE] * (MLA_SCALE * LOG2E)).astype(BF16)
        qm_ref[:, lo + LANE:lo + MLA_QH] = (
            _rope128(qm[:, lo + LANE:lo + MLA_QH], cos, sin) * (MLA_SCALE * LOG2E)).astype(BF16)

    ckv = _rms(zc[:, MLA_Q_RANK:MLA_Q_RANK + MLA_KV_RANK], gkv_ref[...]).astype(BF16)
    kv = jnp.dot(ckv, wukv_ref[...], preferred_element_type=F32)
    for h in range(MLA_HEADS):
        lo = (MLA_NOPE + MLA_V) * h
        kn_ref[:, MLA_QH * h:MLA_QH * h + LANE] = kv[:, lo:lo + MLA_NOPE].astype(BF16)
        kn_ref[:, MLA_QH * h + LANE:MLA_QH * (h + 1)] = kr
        vm_ref[:, MLA_V * h:MLA_V * (h + 1)] = kv[:, lo + MLA_NOPE:lo + MLA_NOPE + MLA_V].astype(BF16)


def _nt_dot(a, b):
    return lax.dot_general(a, b, (((1,), (1,)), ((), ())), preferred_element_type=F32)


def _softmax_pieces(scores):
    m = functools.reduce(jnp.maximum, [jnp.max(s, axis=-1, keepdims=True) for s in scores])
    es = [jnp.exp2(s - m) for s in scores]
    total = functools.reduce(jnp.add, [jnp.sum(e, axis=-1, keepdims=True) for e in es])
    return es, 1.0 / total


def _da_attn_kernel(lam_ref, g_ref, q_ref, *refs, n_pieces, lam_init, sub):
    k_refs, v_refs, o_ref = refs[:n_pieces], refs[n_pieces:2 * n_pieces], refs[2 * n_pieces]
    lv = lam_ref[...]
    lam = (jnp.exp(jnp.sum(lv[0:1] * lv[1:2], axis=-1, keepdims=True))
           - jnp.exp(jnp.sum(lv[2:3] * lv[3:4], axis=-1, keepdims=True)) + lam_init)

    def scores(r):
        q = q_ref[sub * r:sub * (r + 1), :]
        lane = lax.broadcasted_iota(jnp.int32, q.shape, 1)
        zero = jnp.zeros_like(q)
        q0 = jnp.where(lane < DA_QK_DIM, q, zero)
        q1 = jnp.where(lane >= DA_QK_DIM, q, zero)
        return [_nt_dot(q0, k[...]) for k in k_refs], [_nt_dot(q1, k[...]) for k in k_refs]

    def finish(r, s0, s1):
        e0, r0 = _softmax_pieces(s0)
        e1, r1 = _softmax_pieces(s1)
        c = lam * r1 / r0
        o = r0 * functools.reduce(jnp.add, [
            jnp.dot((a - c * b).astype(BF16), v[...], preferred_element_type=F32)
            for a, b, v in zip(e0, e1, v_refs)])
        o_ref[sub * r:sub * (r + 1), :] = (_rms(o, g_ref[...]) * (1.0 - lam_init)).astype(BF16)

    _skewed(q_ref.shape[0] // sub, scores, finish)


def _skewed(n, first, second):
    pending = first(0)
    for r in range(n):
        ahead = first(r + 1) if r + 1 < n else None
        second(r, *pending)
        pending = ahead


def _mla_attn_kernel(q_ref, *refs, n_pieces, sub):
    k_refs, vm_refs, o_ref = refs[:n_pieces], refs[n_pieces:2 * n_pieces], refs[2 * n_pieces]

    def scores(r):
        q = q_ref[sub * r:sub * (r + 1), :]
        return ([_nt_dot(q, k[...]) for k in k_refs],)

    def finish(r, s):
        es, rs = _softmax_pieces(s)
        o = rs * functools.reduce(jnp.add, [
            jnp.dot(e.astype(BF16), v[...], preferred_element_type=F32) for e, v in zip(es, vm_refs)])
        o_ref[sub * r:sub * (r + 1), :] = o.astype(BF16)

    _skewed(q_ref.shape[0] // sub, scores, finish)


def _outproj_kernel(x_ref, gt_ref, a_ref, b_ref, c_ref, w_ref, o_ref):
    r1, r2 = DA_WIDTH, DA_WIDTH + GM_WIDTH
    acc = (jnp.dot(a_ref[...], w_ref[:r1, :], preferred_element_type=F32)
           + jnp.dot(b_ref[...], w_ref[r1:r2, :], preferred_element_type=F32)
           + jnp.dot(c_ref[...], w_ref[r2:, :], preferred_element_type=F32))
    o_ref[...] = x_ref[...] + gt_ref[0] * acc


def _mlp_kernel(x_ref, g_ref, sh_ref, sc_ref, gt_ref, gf_ref, w1_ref, w2_ref, o_ref, h_ref, *, final_norm):
    f = pl.program_id(1)

    @pl.when(f == 0)
    def _():
        _norm_mod_rows(x_ref, g_ref, sh_ref, sc_ref, h_ref)
        o_ref[...] = jnp.zeros_like(o_ref)

    a = jnp.maximum(jnp.dot(h_ref[...], w1_ref[...].astype(BF16), preferred_element_type=F32), 0.0)
    o_ref[...] += jnp.dot((a * a).astype(BF16), w2_ref[...].astype(BF16), preferred_element_type=F32)

    @pl.when(f == pl.num_programs(1) - 1)
    def _():
        gate, gfin = gt_ref[0], gf_ref[...]

        def body(i, carry):
            chunks = _chunk_rows(i)
            outs = [x_ref[rows, :] + gate * o_ref[rows, :] for rows in chunks]
            for rows, out in zip(chunks, outs):
                o_ref[rows, :] = _rms(out, gfin) if final_norm else out
            return carry

        lax.fori_loop(0, o_ref.shape[0] // (NORM_ROWS * NORM_UNROLL), body, 0)


LATENT_ROWS = (0, SEQ)
CONTEXT_ROWS = (CTX_MOD_ROW, None)


def _mod_spec(mod_rows, tm, chunk):
    first, rows_per = mod_rows
    if rows_per is None:
        return pl.BlockSpec((1, 1, D_MODEL), lambda i, *_: (first, 0, chunk))
    return pl.BlockSpec((1, 1, D_MODEL), lambda i, *_: (first + (i * tm) // rows_per, 0, chunk))


def _mod_call(cvec, w_mod, b_mod):
    tn = 1024
    return pl.pallas_call(
        _mod_kernel,
        grid=(DEPTH, 6 * D_MODEL // tn),
        in_specs=[pl.BlockSpec((MOD_ROWS, D_MODEL), lambda l, n: (0, 0)),
                  pl.BlockSpec((1, D_MODEL, tn), lambda l, n: (l, 0, n)),
                  pl.BlockSpec((1, 1, tn), lambda l, n: (l, 0, n))],
        out_specs=pl.BlockSpec((1, MOD_ROWS, tn), lambda l, n: (l, 0, n)),
        out_shape=jax.ShapeDtypeStruct((DEPTH, MOD_ROWS, 6 * D_MODEL), F32),
        compiler_params=_cparams(("arbitrary", "arbitrary")),
        name="mod",
    )(cvec, w_mod, b_mod.reshape(DEPTH, 1, 6 * D_MODEL))


def _mix_in_call(x, g, mod, mod_rows, w, layer, cos_t, sin_t, ggm, wsp, bsp, gq, wuq, gkv, wukv):
    m = x.shape[0]
    tm = 256
    pos_tiles = cos_t.shape[0] // tm
    row = lambda i: (i, 0)
    const2 = lambda i: (0, 0)
    const3 = lambda i: (0, 0, 0)
    widths = (DA_QK_COLS, DA_QK_COLS, DA_WIDTH, GM_WIDTH, MLA_HEADS * MLA_QH,
              MLA_HEADS * MLA_QH, MLA_WIDTH)
    return pl.pallas_call(
        _mix_in_kernel,
        grid=(m // tm,),
        in_specs=[pl.BlockSpec((tm, D_MODEL), row),
                  pl.BlockSpec((1, D_MODEL), const2),
                  _mod_spec(mod_rows, tm, 0), _mod_spec(mod_rows, tm, 1),
                  pl.BlockSpec((None, D_MODEL, Z_COLS), lambda i: (layer, 0, 0), pipeline_mode=pl.Buffered(1)),
                  pl.BlockSpec((tm, LANE), lambda i: (i % pos_tiles, 0)),
                  pl.BlockSpec((tm, LANE), lambda i: (i % pos_tiles, 0)),
                  pl.BlockSpec((1, GM_WIDTH), const2),
                  pl.BlockSpec((GM_GROUPS, GM_CHUNK, GM_CHUNK), const3),
                  pl.BlockSpec((GM_GROUPS, GM_CHUNK, GM_CH), const3),
                  pl.BlockSpec((1, MLA_Q_RANK), const2),
                  pl.BlockSpec((MLA_Q_RANK, MLA_HEADS * MLA_QH), const2),
                  pl.BlockSpec((1, MLA_KV_RANK), const2),
                  pl.BlockSpec((MLA_KV_RANK, MLA_HEADS * (MLA_NOPE + MLA_V)), const2)],
        out_specs=[pl.BlockSpec((tm, w_), row) for w_ in widths],
        out_shape=[jax.ShapeDtypeStruct((m, w_), BF16) for w_ in widths],
        scratch_shapes=[pltpu.VMEM((tm, D_MODEL), BF16)],
        compiler_params=_cparams(("parallel",)),
        name="mix_in",
    )(x, g, mod, mod, w, cos_t, sin_t, ggm, wsp, bsp, gq, wuq, gkv, wukv)


def _kv_specs(arrays, width):
    return [pl.BlockSpec((a.shape[0] // BATCH, width), lambda b, h, t: (b, h)) for a in arrays]


def _da_attn_call(lamv, g, q, ks, vs, lam_init):
    nq = q.shape[0] // BATCH
    tq = min(ATTN_TQ, nq)
    nqt = nq // tq
    qspec = pl.BlockSpec((tq, LANE), lambda b, h, t: (b * nqt + t, h))
    return pl.pallas_call(
        functools.partial(_da_attn_kernel, n_pieces=len(ks), lam_init=lam_init, sub=min(ATTN_SUB, tq)),
        grid=(BATCH, DA_HEADS, nqt),
        in_specs=[pl.BlockSpec(lamv.shape, lambda b, h, t: (0, 0)),
                  pl.BlockSpec((1, DA_V_DIM), lambda b, h, t: (0, 0)),
                  qspec] + _kv_specs(ks, LANE) + _kv_specs(vs, LANE),
        out_specs=qspec,
        out_shape=jax.ShapeDtypeStruct((q.shape[0], DA_WIDTH), BF16),
        compiler_params=_cparams(("parallel", "parallel", "arbitrary")),
        name="da_attn",
    )(lamv, g, q, *ks, *vs)


def _mla_attn_call(qm, ks, vms):
    nq = qm.shape[0] // BATCH
    tq = min(MLA_TQ, nq)
    nqt = nq // tq
    return pl.pallas_call(
        functools.partial(_mla_attn_kernel, n_pieces=len(ks), sub=min(MLA_SUB, tq)),
        grid=(BATCH, MLA_HEADS, nqt),
        in_specs=[pl.BlockSpec((tq, MLA_QH), lambda b, h, t: (b * nqt + t, h))]
        + _kv_specs(ks, MLA_QH) + _kv_specs(vms, LANE),
        out_specs=pl.BlockSpec((tq, MLA_V), lambda b, h, t: (b * nqt + t, h)),
        out_shape=jax.ShapeDtypeStruct((qm.shape[0], MLA_WIDTH), BF16),
        compiler_params=_cparams(("parallel", "parallel", "arbitrary")),
        name="mla_attn",
    )(qm, *ks, *vms)


def _outproj_call(x, mod, mod_rows, o_da, gated, o_mla, w, layer):
    m = x.shape[0]
    tm = 256
    row = lambda i: (i, 0)
    return pl.pallas_call(
        _outproj_kernel,
        grid=(m // tm,),
        in_specs=[pl.BlockSpec((tm, D_MODEL), row), _mod_spec(mod_rows, tm, 2),
                  pl.BlockSpec((tm, DA_WIDTH), row), pl.BlockSpec((tm, GM_WIDTH), row),
                  pl.BlockSpec((tm, MLA_WIDTH), row),
                  pl.BlockSpec((None, MIX_WIDTH, D_MODEL), lambda i: (layer, 0, 0))],
        out_specs=pl.BlockSpec((tm, D_MODEL), row),
        out_shape=jax.ShapeDtypeStruct((m, D_MODEL), F32),
        compiler_params=_cparams(("parallel",)),
        name="outproj",
    )(x, mod, o_da, gated, o_mla, w)


def _mlp_call(x, g, mod, mod_rows, g_final, w1, w2, layer, final_norm):
    m = x.shape[0]
    tm, tf = 1024, 512
    row = lambda i, f: (i, 0)
    return pl.pallas_call(
        functools.partial(_mlp_kernel, final_norm=final_norm),
        grid=(m // tm, D_FF // tf),
        in_specs=[pl.BlockSpec((tm, D_MODEL), row, pipeline_mode=pl.Buffered(1)),
                  pl.BlockSpec((1, D_MODEL), lambda i, f: (0, 0)),
                  _mod_spec(mod_rows, tm, 3), _mod_spec(mod_rows, tm, 4), _mod_spec(mod_rows, tm, 5),
                  pl.BlockSpec((1, D_MODEL), lambda i, f: (0, 0)),
                  pl.BlockSpec((None, D_MODEL, tf), lambda i, f: (layer, 0, f)),
                  pl.BlockSpec((None, tf, D_MODEL), lambda i, f: (layer, f, 0))],
        out_specs=pl.BlockSpec((tm, D_MODEL), row),
        out_shape=jax.ShapeDtypeStruct((m, D_MODEL), F32),
        scratch_shapes=[pltpu.VMEM((tm, D_MODEL), BF16)],
        compiler_params=_cparams(("parallel", "arbitrary")),
        name="mlp",
    )(x, g, mod, mod, mod, g_final, w1, w2)


def kernel(x, c, ctx, c_ctx, w_mod, b_mod, g_norm_mix, g_norm_mlp, w_in, lam_q1, lam_k1, lam_q2, lam_k2,
           g_da_sub, g_gm_v, w_spatial, b_spatial, g_mla_q, w_mla_uq, g_mla_kv, w_mla_ukv, w_out, w_fc1,
           w_fc2, g_final):
    cos_t, sin_t = _rope_tables()
    cos_c = jnp.ones((CTX_LEN, LANE), F32)
    sin_c = jnp.zeros((CTX_LEN, LANE), F32)

    xl = x.reshape(BATCH * SEQ, D_MODEL)
    xc = ctx.reshape(BATCH * CTX_LEN, D_MODEL)
    cvec = jnp.zeros((MOD_ROWS, D_MODEL), F32).at[:BATCH].set(c).at[CTX_MOD_ROW].set(c_ctx)
    mod_all = _mod_call(cvec, w_mod, b_mod)

    g_final2 = g_final.reshape(1, D_MODEL)
    w_in_p = _w_in_layout(w_in)
    w_out_b = w_out.astype(BF16)

    for l in range(DEPTH):
        last = l == DEPTH - 1
        lam_init = 0.8 - 0.6 * math.exp(-0.3 * l)
        mod = mod_all[l].reshape(MOD_ROWS, 1, 6 * D_MODEL)
        wuq_p = _w_uq_layout(w_mla_uq[l])
        wukv_p = w_mla_ukv[l].astype(BF16)
        g_mix, g_mlp = g_norm_mix[l].reshape(1, D_MODEL), g_norm_mlp[l].reshape(1, D_MODEL)
        lamv = jnp.stack([lam_q1[l], lam_k1[l], lam_q2[l], lam_k2[l]])
        g_sub = g_da_sub[l].reshape(1, DA_V_DIM)
        post_w = (g_gm_v[l].reshape(1, GM_WIDTH), w_spatial[l].astype(BF16),
                  jnp.broadcast_to(b_spatial[l][:, :, None], (GM_GROUPS, GM_CHUNK, GM_CH)),
                  g_mla_q[l].reshape(1, MLA_Q_RANK), wuq_p, g_mla_kv[l].reshape(1, MLA_KV_RANK), wukv_p)

        q, k, v, gated, qm, kn, vm = _mix_in_call(xl, g_mix, mod, LATENT_ROWS, w_in_p, l, cos_t, sin_t, *post_w)
        qc, kc, vc, gated_c, qmc, knc, vmc = _mix_in_call(xc, g_mix, mod, CONTEXT_ROWS, w_in_p, l, cos_c, sin_c,
                                                          *post_w)

        o_da = _da_attn_call(lamv, g_sub, q, [k, kc], [v, vc], lam_init)
        o_mla = _mla_attn_call(qm, [kn, knc], [vm, vmc])
        xl = _outproj_call(xl, mod, LATENT_ROWS, o_da, gated, o_mla, w_out_b, l)
        xl = _mlp_call(xl, g_mlp, mod, LATENT_ROWS, g_final2, w_fc1, w_fc2, l, final_norm=last)

        if not last:
            o_da_c = _da_attn_call(lamv, g_sub, qc, [kc], [vc], lam_init)
            o_mla_c = _mla_attn_call(qmc, [knc], [vmc])
            xc = _outproj_call(xc, mod, CONTEXT_ROWS, o_da_c, gated_c, o_mla_c, w_out_b, l)
            xc = _mlp_call(xc, g_mlp, mod, CONTEXT_ROWS, g_final2, w_fc1, w_fc2, l, final_norm=False)

    return xl.reshape(BATCH, SEQ, D_MODEL)
```

```python
import functools
import math

import jax
import jax.numpy as jnp
from jax import lax
from jax.experimental import pallas as pl
from jax.experimental.pallas import tpu as pltpu

D_MODEL = 2048
BATCH = 4
SEQ = 2048
DEPTH = 2
CTX_LEN = 256
GRID_W = 64
EPS = 1e-6
ROPE_THETA = 10000.0

DA_HEADS = 6
DA_QK_DIM = 64
DA_V_DIM = 128
DA_WIDTH = DA_HEADS * DA_V_DIM
DA_QK_COLS = DA_HEADS * 2 * DA_QK_DIM
DA_SCALE = DA_QK_DIM ** -0.5

GM_GROUPS = 4
GM_CH = 128
GM_CHUNK = 128
GM_WIDTH = GM_GROUPS * GM_CH

MLA_HEADS = 6
MLA_Q_RANK = 512
MLA_KV_RANK = 512
MLA_NOPE = 128
MLA_ROPE = 64
MLA_V = 128
MLA_WIDTH = MLA_HEADS * MLA_V
MIX_WIDTH = DA_WIDTH + GM_WIDTH + MLA_WIDTH
MLA_SCALE = (MLA_NOPE + MLA_ROPE) ** -0.5
LOG2E = math.log2(math.e)

ROT_DIM = 64
D_FF = 4 * D_MODEL
MOD_ROWS = 8
CTX_MOD_ROW = BATCH

LANE = 128
Z_Q, Z_K, Z_V, Z_GU, Z_GV, Z_CQ, Z_CKV, Z_KR = 0, 768, 1536, 2304, 2816, 3328, 3840, 4352
Z_COLS = Z_KR + LANE
NORM_ROWS = 16
NORM_UNROLL = 8
MLA_QH = 2 * LANE

VMEM_LIMIT = 56 * 1024 * 1024
ATTN_TQ = 2048
ATTN_SUB = 128
MLA_TQ = 2048
MLA_SUB = 256

BF16 = jnp.bfloat16
F32 = jnp.float32


def _cparams(sem):
    return pltpu.CompilerParams(dimension_semantics=sem, vmem_limit_bytes=VMEM_LIMIT)


def _w_in_layout(w):
    return jnp.pad(w, ((0, 0), (0, 0), (0, Z_COLS - w.shape[-1]))).astype(BF16)


def _w_uq_layout(w):
    k = w.shape[0]
    w = w.reshape(k, MLA_HEADS, MLA_NOPE + MLA_ROPE)
    return jnp.pad(w, ((0, 0), (0, 0), (0, MLA_QH - MLA_NOPE - MLA_ROPE))).reshape(k, -1).astype(BF16)


def _rope_tables():
    rows = SEQ // GRID_W
    row = jnp.repeat(jnp.arange(rows, dtype=F32), GRID_W)
    col = jnp.tile(jnp.arange(GRID_W, dtype=F32), rows)
    n_f = ROT_DIM // 4
    inv = ROPE_THETA ** (-jnp.arange(n_f, dtype=F32) / n_f)
    ang = jnp.concatenate([row[:, None] * inv, col[:, None] * inv], axis=-1)
    cos, sin = jnp.cos(ang), jnp.sin(ang)
    cos_u = jnp.repeat(cos, 2, axis=-1)
    sin_u = jnp.stack([-sin, sin], axis=-1).reshape(SEQ, ROT_DIM)
    return jnp.tile(cos_u, (1, 2)), jnp.tile(sin_u, (1, 2))


def _rms(x, g):
    return x * lax.rsqrt(jnp.mean(x * x, axis=-1, keepdims=True) + EPS) * g


def _gelu(x):
    return 0.5 * x * (1.0 + lax.erf(x * math.sqrt(0.5)))


def _rope128(x, cos, sin):
    lane = lax.broadcasted_iota(jnp.int32, x.shape, 1)
    partner = jnp.where((lane & 1) == 0, pltpu.roll(x, LANE - 1, 1), pltpu.roll(x, 1, 1))
    return x * cos + partner * sin


def _norm_mod_rows(x_ref, g_ref, sh_ref, sc_ref, h_ref):
    gain = g_ref[...] * (1.0 + sc_ref[0])
    shift = sh_ref[0]

    def body(i, carry):
        for rows in _chunk_rows(i):
            x = x_ref[rows, :]
            inv = lax.rsqrt(jnp.mean(x * x, axis=-1, keepdims=True) + EPS)
            h_ref[rows, :] = (x * inv * gain + shift).astype(BF16)
        return carry

    lax.fori_loop(0, x_ref.shape[0] // (NORM_ROWS * NORM_UNROLL), body, 0)


def _chunk_rows(i):
    base = pl.multiple_of(i * (NORM_ROWS * NORM_UNROLL), NORM_ROWS * NORM_UNROLL)
    return [pl.ds(base + NORM_ROWS * k, NORM_ROWS) for k in range(NORM_UNROLL)]


def _mod_kernel(c_ref, w_ref, b_ref, o_ref):
    c = c_ref[...]
    s = (c / (1.0 + jnp.exp(-c))).astype(BF16)
    o_ref[0] = jnp.dot(s, w_ref[0].astype(BF16), preferred_element_type=F32) + b_ref[0]


def _mix_in_kernel(x_ref, g_ref, sh_ref, sc_ref, w_ref, cos_ref, sin_ref, ggm_ref, wsp_ref, bsp_ref,
                   gq_ref, wuq_ref, gkv_ref, wukv_ref,
                   q_ref, k_ref, v_ref, gated_ref, qm_ref, kn_ref, vm_ref, h_ref):
    _norm_mod_rows(x_ref, g_ref, sh_ref, sc_ref, h_ref)
    tm = h_ref.shape[0]
    cos, sin = cos_ref[...], sin_ref[...]

    def seg(lo, hi):
        return jnp.dot(h_ref[...], w_ref[:, lo:hi], preferred_element_type=F32)

    zq = seg(Z_Q, Z_K)
    zk = seg(Z_K, Z_V)
    for j in range(DA_HEADS):
        sl = slice(LANE * j, LANE * (j + 1))
        q_ref[:, sl] = (_rope128(zq[:, sl], cos, sin) * (DA_SCALE * LOG2E)).astype(BF16)
    zv = seg(Z_V, Z_GU)
    for j in range(DA_HEADS):
        sl = slice(LANE * j, LANE * (j + 1))
        k_ref[:, sl] = _rope128(zk[:, sl], cos, sin).astype(BF16)
    zg = seg(Z_GU, Z_CQ)
    v_ref[...] = zv.astype(BF16)
    zc = seg(Z_CQ, Z_COLS)

    for g in range(GM_GROUPS):
        sl = slice(GM_CH * g, GM_CH * (g + 1))
        u = _gelu(zg[:, GM_CH * g:GM_CH * (g + 1)])
        v = _rms(_gelu(zg[:, GM_WIDTH + GM_CH * g:GM_WIDTH + GM_CH * (g + 1)]), ggm_ref[:, sl]).astype(BF16)
        for c in range(tm // GM_CHUNK):
            rows = slice(GM_CHUNK * c, GM_CHUNK * (c + 1))
            mixed = jnp.dot(wsp_ref[g], v[rows], preferred_element_type=F32) + bsp_ref[g]
            gated_ref[rows, sl] = (u[rows] * mixed).astype(BF16)

    kr = _rope128(zc[:, Z_KR - Z_CQ:], cos, sin).astype(BF16)
    cq = _rms(zc[:, :MLA_Q_RANK], gq_ref[...]).astype(BF16)
    qm = jnp.dot(cq, wuq_ref[...], preferred_element_type=F32)
    for h in range(MLA_HEADS):
        lo = MLA_QH * h
        qm_ref[:, lo:lo + LANE] = (qm[:, lo:lo + LANE] * (MLA_SCALE * LOG2E)).astype(BF16)
        qm_ref[:, lo + LANE:lo + MLA_QH] = (
            _rope128(qm[:, lo + LANE:lo + MLA_QH], cos, sin) * (MLA_SCALE * LOG2E)).astype(BF16)

    ckv = _rms(zc[:, MLA_Q_RANK:MLA_Q_RANK + MLA_KV_RANK], gkv_ref[...]).astype(BF16)
    kv = jnp.dot(ckv, wukv_ref[...], preferred_element_type=F32)
    for h in range(MLA_HEADS):
        lo = (MLA_NOPE + MLA_V) * h
        kn_ref[:, MLA_QH * h:MLA_QH * h + LANE] = kv[:, lo:lo + MLA_NOPE].astype(BF16)
        kn_ref[:, MLA_QH * h + LANE:MLA_QH * (h + 1)] = kr
        vm_ref[:, MLA_V * h:MLA_V * (h + 1)] = kv[:, lo + MLA_NOPE:lo + MLA_NOPE + MLA_V].astype(BF16)


def _nt_dot(a, b):
    return lax.dot_general(a, b, (((1,), (1,)), ((), ())), preferred_element_type=F32)


def _softmax_pieces(scores):
    m = functools.reduce(jnp.maximum, [jnp.max(s, axis=-1, keepdims=True) for s in scores])
    es = [jnp.exp2(s - m) for s in scores]
    total = functools.reduce(jnp.add, [jnp.sum(e, axis=-1, keepdims=True) for e in es])
    return es, 1.0 / total


def _da_attn_kernel(lam_ref, g_ref, q_ref, *refs, n_pieces, lam_init, sub):
    k_refs, v_refs, o_ref = refs[:n_pieces], refs[n_pieces:2 * n_pieces], refs[2 * n_pieces]
    lv = lam_ref[...]
    lam = (jnp.exp(jnp.sum(lv[0:1] * lv[1:2], axis=-1, keepdims=True))
           - jnp.exp(jnp.sum(lv[2:3] * lv[3:4], axis=-1, keepdims=True)) + lam_init)

    def scores(r):
        q = q_ref[sub * r:sub * (r + 1), :]
        lane = lax.broadcasted_iota(jnp.int32, q.shape, 1)
        zero = jnp.zeros_like(q)
        q0 = jnp.where(lane < DA_QK_DIM, q, zero)
        q1 = jnp.where(lane >= DA_QK_DIM, q, zero)
        return [_nt_dot(q0, k[...]) for k in k_refs], [_nt_dot(q1, k[...]) for k in k_refs]

    def finish(r, s0, s1):
        e0, r0 = _softmax_pieces(s0)
        e1, r1 = _softmax_pieces(s1)
        c = lam * r1 / r0
        o = r0 * functools.reduce(jnp.add, [
            jnp.dot((a - c * b).astype(BF16), v[...], preferred_element_type=F32)
            for a, b, v in zip(e0, e1, v_refs)])
        o_ref[sub * r:sub * (r + 1), :] = (_rms(o, g_ref[...]) * (1.0 - lam_init)).astype(BF16)

    _skewed(q_ref.shape[0] // sub, scores, finish)


def _skewed(n, first, second):
    pending = first(0)
    for r in range(n):
        ahead = first(r + 1) if r + 1 < n else None
        second(r, *pending)
        pending = ahead


def _mla_attn_kernel(q_ref, *refs, n_pieces, sub):
    k_refs, vm_refs, o_ref = refs[:n_pieces], refs[n_pieces:2 * n_pieces], refs[2 * n_pieces]

    def scores(r):
        q = q_ref[sub * r:sub * (r + 1), :]
        return ([_nt_dot(q, k[...]) for k in k_refs],)

    def finish(r, s):
        es, rs = _softmax_pieces(s)
        o = rs * functools.reduce(jnp.add, [
            jnp.dot(e.astype(BF16), v[...], preferred_element_type=F32) for e, v in zip(es, vm_refs)])
        o_ref[sub * r:sub * (r + 1), :] = o.astype(BF16)

    _skewed(q_ref.shape[0] // sub, scores, finish)


def _outproj_kernel(x_ref, gt_ref, a_ref, b_ref, c_ref, w32_ref, o_ref, w_ref):
    @pl.when(pl.program_id(0) == 0)
    def _():
        w_ref[...] = w32_ref[...].astype(BF16)

    r1, r2 = DA_WIDTH, DA_WIDTH + GM_WIDTH
    acc = (jnp.dot(a_ref[...], w_ref[:r1, :], preferred_element_type=F32)
           + jnp.dot(b_ref[...], w_ref[r1:r2, :], preferred_element_type=F32)
           + jnp.dot(c_ref[...], w_ref[r2:, :], preferred_element_type=F32))
    o_ref[...] = x_ref[...] + gt_ref[0] * acc


def _mlp_kernel(x_ref, g_ref, sh_ref, sc_ref, gt_ref, gf_ref, w1_ref, w2_ref, o_ref, h_ref, *, final_norm):
    f = pl.program_id(1)

    @pl.when(f == 0)
    def _():
        _norm_mod_rows(x_ref, g_ref, sh_ref, sc_ref, h_ref)
        o_ref[...] = jnp.zeros_like(o_ref)

    a = jnp.maximum(jnp.dot(h_ref[...], w1_ref[...].astype(BF16), preferred_element_type=F32), 0.0)
    o_ref[...] += jnp.dot((a * a).astype(BF16), w2_ref[...].astype(BF16), preferred_element_type=F32)

    @pl.when(f == pl.num_programs(1) - 1)
    def _():
        gate, gfin = gt_ref[0], gf_ref[...]

        def body(i, carry):
            chunks = _chunk_rows(i)
            outs = [x_ref[rows, :] + gate * o_ref[rows, :] for rows in chunks]
            for rows, out in zip(chunks, outs):
                o_ref[rows, :] = _rms(out, gfin) if final_norm else out
            return carry

        lax.fori_loop(0, o_ref.shape[0] // (NORM_ROWS * NORM_UNROLL), body, 0)


LATENT_ROWS = (0, SEQ)
CONTEXT_ROWS = (CTX_MOD_ROW, None)


def _mod_spec(mod_rows, tm, chunk):
    first, rows_per = mod_rows
    if rows_per is None:
        return pl.BlockSpec((1, 1, D_MODEL), lambda i, *_: (first, 0, chunk))
    return pl.BlockSpec((1, 1, D_MODEL), lambda i, *_: (first + (i * tm) // rows_per, 0, chunk))


def _mod_call(cvec, w_mod, b_mod):
    tn = 1024
    return pl.pallas_call(
        _mod_kernel,
        grid=(DEPTH, 6 * D_MODEL // tn),
        in_specs=[pl.BlockSpec((MOD_ROWS, D_MODEL), lambda l, n: (0, 0)),
                  pl.BlockSpec((1, D_MODEL, tn), lambda l, n: (l, 0, n)),
                  pl.BlockSpec((1, 1, tn), lambda l, n: (l, 0, n))],
        out_specs=pl.BlockSpec((1, MOD_ROWS, tn), lambda l, n: (l, 0, n)),
        out_shape=jax.ShapeDtypeStruct((DEPTH, MOD_ROWS, 6 * D_MODEL), F32),
        compiler_params=_cparams(("arbitrary", "arbitrary")),
        name="mod",
    )(cvec, w_mod, b_mod.reshape(DEPTH, 1, 6 * D_MODEL))


def _mix_in_call(x, g, mod, mod_rows, w, layer, cos_t, sin_t, ggm, wsp, bsp, gq, wuq, gkv, wukv):
    m = x.shape[0]
    tm = 256
    pos_tiles = cos_t.shape[0] // tm
    row = lambda i: (i, 0)
    const2 = lambda i: (0, 0)
    const3 = lambda i: (0, 0, 0)
    widths = (DA_QK_COLS, DA_QK_COLS, DA_WIDTH, GM_WIDTH, MLA_HEADS * MLA_QH,
              MLA_HEADS * MLA_QH, MLA_WIDTH)
    return pl.pallas_call(
        _mix_in_kernel,
        grid=(m // tm,),
        in_specs=[pl.BlockSpec((tm, D_MODEL), row),
                  pl.BlockSpec((1, D_MODEL), const2),
                  _mod_spec(mod_rows, tm, 0), _mod_spec(mod_rows, tm, 1),
                  pl.BlockSpec((None, D_MODEL, Z_COLS), lambda i: (layer, 0, 0), pipeline_mode=pl.Buffered(1)),
                  pl.BlockSpec((tm, LANE), lambda i: (i % pos_tiles, 0)),
                  pl.BlockSpec((tm, LANE), lambda i: (i % pos_tiles, 0)),
                  pl.BlockSpec((1, GM_WIDTH), const2),
                  pl.BlockSpec((GM_GROUPS, GM_CHUNK, GM_CHUNK), const3),
                  pl.BlockSpec((GM_GROUPS, GM_CHUNK, GM_CH), const3),
                  pl.BlockSpec((1, MLA_Q_RANK), const2),
                  pl.BlockSpec((MLA_Q_RANK, MLA_HEADS * MLA_QH), const2),
                  pl.BlockSpec((1, MLA_KV_RANK), const2),
                  pl.BlockSpec((MLA_KV_RANK, MLA_HEADS * (MLA_NOPE + MLA_V)), const2)],
        out_specs=[pl.BlockSpec((tm, w_), row) for w_ in widths],
        out_shape=[jax.ShapeDtypeStruct((m, w_), BF16) for w_ in widths],
        scratch_shapes=[pltpu.VMEM((tm, D_MODEL), BF16)],
        compiler_params=_cparams(("parallel",)),
        name="mix_in",
    )(x, g, mod, mod, w, cos_t, sin_t, ggm, wsp, bsp, gq, wuq, gkv, wukv)


def _kv_specs(arrays, width):
    return [pl.BlockSpec((a.shape[0] // BATCH, width), lambda b, h, t: (b, h)) for a in arrays]


def _da_attn_call(lamv, g, q, ks, vs, lam_init):
    nq = q.shape[0] // BATCH
    tq = min(ATTN_TQ, nq)
    nqt = nq // tq
    qspec = pl.BlockSpec((tq, LANE), lambda b, h, t: (b * nqt + t, h))
    return pl.pallas_call(
        functools.partial(_da_attn_kernel, n_pieces=len(ks), lam_init=lam_init, sub=min(ATTN_SUB, tq)),
        grid=(BATCH, DA_HEADS, nqt),
        in_specs=[pl.BlockSpec(lamv.shape, lambda b, h, t: (0, 0)),
                  pl.BlockSpec((1, DA_V_DIM), lambda b, h, t: (0, 0)),
                  qspec] + _kv_specs(ks, LANE) + _kv_specs(vs, LANE),
        out_specs=qspec,
        out_shape=jax.ShapeDtypeStruct((q.shape[0], DA_WIDTH), BF16),
        compiler_params=_cparams(("parallel", "parallel", "arbitrary")),
        name="da_attn",
    )(lamv, g, q, *ks, *vs)


def _mla_attn_call(qm, ks, vms):
    nq = qm.shape[0] // BATCH
    tq = min(MLA_TQ, nq)
    nqt = nq // tq
    return pl.pallas_call(
        functools.partial(_mla_attn_kernel, n_pieces=len(ks), sub=min(MLA_SUB, tq)),
        grid=(BATCH, MLA_HEADS, nqt),
        in_specs=[pl.BlockSpec((tq, MLA_QH), lambda b, h, t: (b * nqt + t, h))]
        + _kv_specs(ks, MLA_QH) + _kv_specs(vms, LANE),
        out_specs=pl.BlockSpec((tq, MLA_V), lambda b, h, t: (b * nqt + t, h)),
        out_shape=jax.ShapeDtypeStruct((qm.shape[0], MLA_WIDTH), BF16),
        compiler_params=_cparams(("parallel", "parallel", "arbitrary")),
        name="mla_attn",
    )(qm, *ks, *vms)


def _outproj_call(x, mod, mod_rows, o_da, gated, o_mla, w, layer):
    m = x.shape[0]
    tm = 256
    row = lambda i: (i, 0)
    return pl.pallas_call(
        _outproj_kernel,
        grid=(m // tm,),
        in_specs=[pl.BlockSpec((tm, D_MODEL), row), _mod_spec(mod_rows, tm, 2),
                  pl.BlockSpec((tm, DA_WIDTH), row), pl.BlockSpec((tm, GM_WIDTH), row),
                  pl.BlockSpec((tm, MLA_WIDTH), row),
                  pl.BlockSpec((None, MIX_WIDTH, D_MODEL), lambda i: (layer, 0, 0),
                               pipeline_mode=pl.Buffered(1))],
        out_specs=pl.BlockSpec((tm, D_MODEL), row),
        out_shape=jax.ShapeDtypeStruct((m, D_MODEL), F32),
        scratch_shapes=[pltpu.VMEM((MIX_WIDTH, D_MODEL), BF16)],
        compiler_params=_cparams(("arbitrary",)),
        name="outproj",
    )(x, mod, o_da, gated, o_mla, w)


def _mlp_call(x, g, mod, mod_rows, g_final, w1, w2, layer, final_norm):
    m = x.shape[0]
    tm, tf = 1024, 512
    row = lambda i, f: (i, 0)
    return pl.pallas_call(
        functools.partial(_mlp_kernel, final_norm=final_norm),
        grid=(m // tm, D_FF // tf),
        in_specs=[pl.BlockSpec((tm, D_MODEL), row, pipeline_mode=pl.Buffered(1)),
                  pl.BlockSpec((1, D_MODEL), lambda i, f: (0, 0)),
                  _mod_spec(mod_rows, tm, 3), _mod_spec(mod_rows, tm, 4), _mod_spec(mod_rows, tm, 5),
                  pl.BlockSpec((1, D_MODEL), lambda i, f: (0, 0)),
                  pl.BlockSpec((None, D_MODEL, tf), lambda i, f: (layer, 0, f)),
                  pl.BlockSpec((None, tf, D_MODEL), lambda i, f: (layer, f, 0))],
        out_specs=pl.BlockSpec((tm, D_MODEL), row),
        out_shape=jax.ShapeDtypeStruct((m, D_MODEL), F32),
        scratch_shapes=[pltpu.VMEM((tm, D_MODEL), BF16)],
        compiler_params=_cparams(("parallel", "arbitrary")),
        name="mlp",
    )(x, g, mod, mod, mod, g_final, w1, w2)


def kernel(x, c, ctx, c_ctx, w_mod, b_mod, g_norm_mix, g_norm_mlp, w_in, lam_q1, lam_k1, lam_q2, lam_k2,
           g_da_sub, g_gm_v, w_spatial, b_spatial, g_mla_q, w_mla_uq, g_mla_kv, w_mla_ukv, w_out, w_fc1,
           w_fc2, g_final):
    cos_t, sin_t = _rope_tables()
    cos_c = jnp.ones((CTX_LEN, LANE), F32)
    sin_c = jnp.zeros((CTX_LEN, LANE), F32)

    xl = x.reshape(BATCH * SEQ, D_MODEL)
    xc = ctx.reshape(BATCH * CTX_LEN, D_MODEL)
    cvec = jnp.zeros((MOD_ROWS, D_MODEL), F32).at[:BATCH].set(c).at[CTX_MOD_ROW].set(c_ctx)
    mod_all = _mod_call(cvec, w_mod, b_mod)

    g_final2 = g_final.reshape(1, D_MODEL)
    w_in_p = _w_in_layout(w_in)

    for l in range(DEPTH):
        last = l == DEPTH - 1
        lam_init = 0.8 - 0.6 * math.exp(-0.3 * l)
        mod = mod_all[l].reshape(MOD_ROWS, 1, 6 * D_MODEL)
        wuq_p = _w_uq_layout(w_mla_uq[l])
        wukv_p = w_mla_ukv[l].astype(BF16)
        g_mix, g_mlp = g_norm_mix[l].reshape(1, D_MODEL), g_norm_mlp[l].reshape(1, D_MODEL)
        lamv = jnp.stack([lam_q1[l], lam_k1[l], lam_q2[l], lam_k2[l]])
        g_sub = g_da_sub[l].reshape(1, DA_V_DIM)
        post_w = (g_gm_v[l].reshape(1, GM_WIDTH), w_spatial[l].astype(BF16),
                  jnp.broadcast_to(b_spatial[l][:, :, None], (GM_GROUPS, GM_CHUNK, GM_CH)),
                  g_mla_q[l].reshape(1, MLA_Q_RANK), wuq_p, g_mla_kv[l].reshape(1, MLA_KV_RANK), wukv_p)

        q, k, v, gated, qm, kn, vm = _mix_in_call(xl, g_mix, mod, LATENT_ROWS, w_in_p, l, cos_t, sin_t, *post_w)
        qc, kc, vc, gated_c, qmc, knc, vmc = _mix_in_call(xc, g_mix, mod, CONTEXT_ROWS, w_in_p, l, cos_c, sin_c,
                                                          *post_w)

        o_da = _da_attn_call(lamv, g_sub, q, [k, kc], [v, vc], lam_init)
        o_mla = _mla_attn_call(qm, [kn, knc], [vm, vmc])
        xl = _outproj_call(xl, mod, LATENT_ROWS, o_da, gated, o_mla, w_out, l)
        xl = _mlp_call(xl, g_mlp, mod, LATENT_ROWS, g_final2, w_fc1, w_fc2, l, final_norm=last)

        if not last:
            o_da_c = _da_attn_call(lamv, g_sub, qc, [kc], [vc], lam_init)
            o_mla_c = _mla_attn_call(qmc, [knc], [vmc])
            xc = _outproj_call(xc, mod, CONTEXT_ROWS, o_da_c, gated_c, o_mla_c, w_out, l)
            xc = _mlp_call(xc, g_mlp, mod, CONTEXT_ROWS, g_final2, w_fc1, w_fc2, l, final_norm=False)

    return xl.reshape(BATCH, SEQ, D_MODEL)
```

```python
import functools
import math

import jax
import jax.numpy as jnp
from jax import lax
from jax.experimental import pallas as pl
from jax.experimental.pallas import tpu as pltpu

D_MODEL = 2048
BATCH = 4
SEQ = 2048
DEPTH = 2
CTX_LEN = 256
GRID_W = 64
EPS = 1e-6
ROPE_THETA = 10000.0

DA_HEADS = 6
DA_QK_DIM = 64
DA_V_DIM = 128
DA_WIDTH = DA_HEADS * DA_V_DIM
DA_QK_COLS = DA_HEADS * 2 * DA_QK_DIM
DA_SCALE = DA_QK_DIM ** -0.5

GM_GROUPS = 4
GM_CH = 128
GM_CHUNK = 128
GM_WIDTH = GM_GROUPS * GM_CH

MLA_HEADS = 6
MLA_Q_RANK = 512
MLA_KV_RANK = 512
MLA_NOPE = 128
MLA_ROPE = 64
MLA_V = 128
MLA_WIDTH = MLA_HEADS * MLA_V
MIX_WIDTH = DA_WIDTH + GM_WIDTH + MLA_WIDTH
MLA_SCALE = (MLA_NOPE + MLA_ROPE) ** -0.5
LOG2E = math.log2(math.e)

ROT_DIM = 64
D_FF = 4 * D_MODEL
MOD_ROWS = 8
CTX_MOD_ROW = BATCH

LANE = 128
Z_Q, Z_K, Z_V, Z_GU, Z_GV, Z_CQ, Z_CKV, Z_KR = 0, 768, 1536, 2304, 2816, 3328, 3840, 4352
IN_COLS = Z_KR + MLA_ROPE
NORM_ROWS = 16
NORM_UNROLL = 8
MLA_QH = 2 * LANE

VMEM_LIMIT = 56 * 1024 * 1024
ATTN_TQ = 2048
ATTN_SUB = 128
MLA_TQ = 2048
MLA_SUB = 256

BF16 = jnp.bfloat16
F32 = jnp.float32


def _cparams(sem):
    return pltpu.CompilerParams(dimension_semantics=sem, vmem_limit_bytes=VMEM_LIMIT)


def _w_kr_layout(w):
    return jnp.pad(w[:, :, Z_KR:], ((0, 0), (0, 0), (0, LANE - MLA_ROPE))).astype(BF16)


def _w_uq_layout(w):
    k = w.shape[0]
    w = w.reshape(k, MLA_HEADS, MLA_NOPE + MLA_ROPE)
    return jnp.pad(w, ((0, 0), (0, 0), (0, MLA_QH - MLA_NOPE - MLA_ROPE))).reshape(k, -1).astype(BF16)


def _rope_tables():
    rows = SEQ // GRID_W
    row = jnp.repeat(jnp.arange(rows, dtype=F32), GRID_W)
    col = jnp.tile(jnp.arange(GRID_W, dtype=F32), rows)
    n_f = ROT_DIM // 4
    inv = ROPE_THETA ** (-jnp.arange(n_f, dtype=F32) / n_f)
    ang = jnp.concatenate([row[:, None] * inv, col[:, None] * inv], axis=-1)
    cos, sin = jnp.cos(ang), jnp.sin(ang)
    cos_u = jnp.repeat(cos, 2, axis=-1)
    sin_u = jnp.stack([-sin, sin], axis=-1).reshape(SEQ, ROT_DIM)
    return jnp.tile(cos_u, (1, 2)), jnp.tile(sin_u, (1, 2))


def _rms(x, g):
    return x * lax.rsqrt(jnp.mean(x * x, axis=-1, keepdims=True) + EPS) * g


def _gelu(x):
    return 0.5 * x * (1.0 + lax.erf(x * math.sqrt(0.5)))


def _rope128(x, cos, sin):
    lane = lax.broadcasted_iota(jnp.int32, x.shape, 1)
    partner = jnp.where((lane & 1) == 0, pltpu.roll(x, LANE - 1, 1), pltpu.roll(x, 1, 1))
    return x * cos + partner * sin


def _norm_mod_rows(x_ref, g_ref, sh_ref, sc_ref, h_ref):
    gain = g_ref[...] * (1.0 + sc_ref[0])
    shift = sh_ref[0]

    def body(i, carry):
        for rows in _chunk_rows(i):
            x = x_ref[rows, :]
            inv = lax.rsqrt(jnp.mean(x * x, axis=-1, keepdims=True) + EPS)
            h_ref[rows, :] = (x * inv * gain + shift).astype(BF16)
        return carry

    lax.fori_loop(0, x_ref.shape[0] // (NORM_ROWS * NORM_UNROLL), body, 0)


def _chunk_rows(i):
    base = pl.multiple_of(i * (NORM_ROWS * NORM_UNROLL), NORM_ROWS * NORM_UNROLL)
    return [pl.ds(base + NORM_ROWS * k, NORM_ROWS) for k in range(NORM_UNROLL)]


def _mod_kernel(c_ref, w_ref, b_ref, o_ref):
    c = c_ref[...]
    s = (c / (1.0 + jnp.exp(-c))).astype(BF16)
    o_ref[0] = jnp.dot(s, w_ref[0].astype(BF16), preferred_element_type=F32) + b_ref[0]


def _mix_in_kernel(x_ref, g_ref, sh_ref, sc_ref, w_ref, wkr_ref, cos_ref, sin_ref, ggm_ref, wsp_ref, bsp_ref,
                   gq_ref, wuq_ref, gkv_ref, wukv_ref,
                   q_ref, k_ref, v_ref, gated_ref, qm_ref, kn_ref, vm_ref, h_ref):
    _norm_mod_rows(x_ref, g_ref, sh_ref, sc_ref, h_ref)
    tm = h_ref.shape[0]
    cos, sin = cos_ref[...], sin_ref[...]

    def seg(lo, hi):
        return jnp.dot(h_ref[...], w_ref[:, lo:hi], preferred_element_type=F32)

    zq = seg(Z_Q, Z_K)
    zk = seg(Z_K, Z_V)
    for j in range(DA_HEADS):
        sl = slice(LANE * j, LANE * (j + 1))
        q_ref[:, sl] = (_rope128(zq[:, sl], cos, sin) * (DA_SCALE * LOG2E)).astype(BF16)
    zv = seg(Z_V, Z_GU)
    for j in range(DA_HEADS):
        sl = slice(LANE * j, LANE * (j + 1))
        k_ref[:, sl] = _rope128(zk[:, sl], cos, sin).astype(BF16)
    zg = seg(Z_GU, Z_CQ)
    v_ref[...] = zv.astype(BF16)
    zc = seg(Z_CQ, Z_KR)
    zkr = jnp.dot(h_ref[...], wkr_ref[...], preferred_element_type=F32)

    for g in range(GM_GROUPS):
        sl = slice(GM_CH * g, GM_CH * (g + 1))
        u = _gelu(zg[:, GM_CH * g:GM_CH * (g + 1)])
        v = _rms(_gelu(zg[:, GM_WIDTH + GM_CH * g:GM_WIDTH + GM_CH * (g + 1)]), ggm_ref[:, sl]).astype(BF16)
        for c in range(tm // GM_CHUNK):
            rows = slice(GM_CHUNK * c, GM_CHUNK * (c + 1))
            mixed = jnp.dot(wsp_ref[g], v[rows], preferred_element_type=F32) + bsp_ref[g]
            gated_ref[rows, sl] = (u[rows] * mixed).astype(BF16)

    kr = _rope128(zkr, cos, sin).astype(BF16)
    cq = _rms(zc[:, :MLA_Q_RANK], gq_ref[...]).astype(BF16)
    qm = jnp.dot(cq, wuq_ref[...], preferred_element_type=F32)
    for h in range(MLA_HEADS):
        lo = MLA_QH * h
        qm_ref[:, lo:lo + LANE] = (qm[:, lo:lo + LANE] * (MLA_SCALE * LOG2E)).astype(BF16)
        qm_ref[:, lo + LANE:lo + MLA_QH] = (
            _rope128(qm[:, lo + LANE:lo + MLA_QH], cos, sin) * (MLA_SCALE * LOG2E)).astype(BF16)

    ckv = _rms(zc[:, MLA_Q_RANK:MLA_Q_RANK + MLA_KV_RANK], gkv_ref[...]).astype(BF16)
    kv = jnp.dot(ckv, wukv_ref[...], preferred_element_type=F32)
    for h in range(MLA_HEADS):
        lo = (MLA_NOPE + MLA_V) * h
        kn_ref[:, MLA_QH * h:MLA_QH * h + LANE] = kv[:, lo:lo + MLA_NOPE].astype(BF16)
        kn_ref[:, MLA_QH * h + LANE:MLA_QH * (h + 1)] = kr
        vm_ref[:, MLA_V * h:MLA_V * (h + 1)] = kv[:, lo + MLA_NOPE:lo + MLA_NOPE + MLA_V].astype(BF16)


def _nt_dot(a, b):
    return lax.dot_general(a, b, (((1,), (1,)), ((), ())), preferred_element_type=F32)


def _softmax_pieces(scores):
    m = functools.reduce(jnp.maximum, [jnp.max(s, axis=-1, keepdims=True) for s in scores])
    es = [jnp.exp2(s - m) for s in scores]
    total = functools.reduce(jnp.add, [jnp.sum(e, axis=-1, keepdims=True) for e in es])
    return es, 1.0 / total


def _da_attn_kernel(lam_ref, g_ref, q_ref, *refs, n_pieces, lam_init, sub):
    k_refs, v_refs, o_ref = refs[:n_pieces], refs[n_pieces:2 * n_pieces], refs[2 * n_pieces]
    lv = lam_ref[...]
    lam = (jnp.exp(jnp.sum(lv[0:1] * lv[1:2], axis=-1, keepdims=True))
           - jnp.exp(jnp.sum(lv[2:3] * lv[3:4], axis=-1, keepdims=True)) + lam_init)

    def scores(r):
        q = q_ref[sub * r:sub * (r + 1), :]
        lane = lax.broadcasted_iota(jnp.int32, q.shape, 1)
        zero = jnp.zeros_like(q)
        q0 = jnp.where(lane < DA_QK_DIM, q, zero)
        q1 = jnp.where(lane >= DA_QK_DIM, q, zero)
        return [_nt_dot(q0, k[...]) for k in k_refs], [_nt_dot(q1, k[...]) for k in k_refs]

    def finish(r, s0, s1):
        e0, r0 = _softmax_pieces(s0)
        e1, r1 = _softmax_pieces(s1)
        c = lam * r1 / r0
        o = r0 * functools.reduce(jnp.add, [
            jnp.dot((a - c * b).astype(BF16), v[...], preferred_element_type=F32)
            for a, b, v in zip(e0, e1, v_refs)])
        o_ref[sub * r:sub * (r + 1), :] = (_rms(o, g_ref[...]) * (1.0 - lam_init)).astype(BF16)

    _skewed(q_ref.shape[0] // sub, scores, finish)


def _skewed(n, first, second):
    pending = first(0)
    for r in range(n):
        ahead = first(r + 1) if r + 1 < n else None
        second(r, *pending)
        pending = ahead


def _mla_attn_kernel(q_ref, *refs, n_pieces, sub):
    k_refs, vm_refs, o_ref = refs[:n_pieces], refs[n_pieces:2 * n_pieces], refs[2 * n_pieces]

    def scores(r):
        q = q_ref[sub * r:sub * (r + 1), :]
        return ([_nt_dot(q, k[...]) for k in k_refs],)

    def finish(r, s):
        es, rs = _softmax_pieces(s)
        o = rs * functools.reduce(jnp.add, [
            jnp.dot(e.astype(BF16), v[...], preferred_element_type=F32) for e, v in zip(es, vm_refs)])
        o_ref[sub * r:sub * (r + 1), :] = o.astype(BF16)

    _skewed(q_ref.shape[0] // sub, scores, finish)


def _outproj_kernel(x_ref, gt_ref, a_ref, b_ref, c_ref, w32_ref, o_ref, w_ref):
    @pl.when(pl.program_id(0) == 0)
    def _():
        w_ref[...] = w32_ref[...].astype(BF16)

    r1, r2 = DA_WIDTH, DA_WIDTH + GM_WIDTH
    acc = (jnp.dot(a_ref[...], w_ref[:r1, :], preferred_element_type=F32)
           + jnp.dot(b_ref[...], w_ref[r1:r2, :], preferred_element_type=F32)
           + jnp.dot(c_ref[...], w_ref[r2:, :], preferred_element_type=F32))
    o_ref[...] = x_ref[...] + gt_ref[0] * acc


def _mlp_kernel(x_ref, g_ref, sh_ref, sc_ref, gt_ref, gf_ref, w1_ref, w2_ref, o_ref, h_ref, *, final_norm):
    f = pl.program_id(1)

    @pl.when(f == 0)
    def _():
        _norm_mod_rows(x_ref, g_ref, sh_ref, sc_ref, h_ref)
        o_ref[...] = jnp.zeros_like(o_ref)

    a = jnp.maximum(jnp.dot(h_ref[...], w1_ref[...].astype(BF16), preferred_element_type=F32), 0.0)
    o_ref[...] += jnp.dot((a * a).astype(BF16), w2_ref[...].astype(BF16), preferred_element_type=F32)

    @pl.when(f == pl.num_programs(1) - 1)
    def _():
        gate, gfin = gt_ref[0], gf_ref[...]

        def body(i, carry):
            chunks = _chunk_rows(i)
            outs = [x_ref[rows, :] + gate * o_ref[rows, :] for rows in chunks]
            for rows, out in zip(chunks, outs):
                o_ref[rows, :] = _rms(out, gfin) if final_norm else out
            return carry

        lax.fori_loop(0, o_ref.shape[0] // (NORM_ROWS * NORM_UNROLL), body, 0)


LATENT_ROWS = (0, SEQ)
CONTEXT_ROWS = (CTX_MOD_ROW, None)


def _mod_spec(mod_rows, tm, chunk):
    first, rows_per = mod_rows
    if rows_per is None:
        return pl.BlockSpec((1, 1, D_MODEL), lambda i, *_: (first, 0, chunk))
    return pl.BlockSpec((1, 1, D_MODEL), lambda i, *_: (first + (i * tm) // rows_per, 0, chunk))


def _mod_call(cvec, w_mod, b_mod):
    tn = 1024
    return pl.pallas_call(
        _mod_kernel,
        grid=(DEPTH, 6 * D_MODEL // tn),
        in_specs=[pl.BlockSpec((MOD_ROWS, D_MODEL), lambda l, n: (0, 0)),
                  pl.BlockSpec((1, D_MODEL, tn), lambda l, n: (l, 0, n)),
                  pl.BlockSpec((1, 1, tn), lambda l, n: (l, 0, n))],
        out_specs=pl.BlockSpec((1, MOD_ROWS, tn), lambda l, n: (l, 0, n)),
        out_shape=jax.ShapeDtypeStruct((DEPTH, MOD_ROWS, 6 * D_MODEL), F32),
        compiler_params=_cparams(("arbitrary", "arbitrary")),
        name="mod",
    )(cvec, w_mod, b_mod.reshape(DEPTH, 1, 6 * D_MODEL))


def _mix_in_call(x, g, mod, mod_rows, w, wkr, layer, cos_t, sin_t, ggm, wsp, bsp, gq, wuq, gkv, wukv):
    m = x.shape[0]
    tm = 256
    pos_tiles = cos_t.shape[0] // tm
    row = lambda i: (i, 0)
    const2 = lambda i: (0, 0)
    const3 = lambda i: (0, 0, 0)
    widths = (DA_QK_COLS, DA_QK_COLS, DA_WIDTH, GM_WIDTH, MLA_HEADS * MLA_QH,
              MLA_HEADS * MLA_QH, MLA_WIDTH)
    return pl.pallas_call(
        _mix_in_kernel,
        grid=(m // tm,),
        in_specs=[pl.BlockSpec((tm, D_MODEL), row),
                  pl.BlockSpec((1, D_MODEL), const2),
                  _mod_spec(mod_rows, tm, 0), _mod_spec(mod_rows, tm, 1),
                  pl.BlockSpec((None, D_MODEL, IN_COLS), lambda i: (layer, 0, 0), pipeline_mode=pl.Buffered(1)),
                  pl.BlockSpec((None, D_MODEL, LANE), lambda i: (layer, 0, 0)),
                  pl.BlockSpec((tm, LANE), lambda i: (i % pos_tiles, 0)),
                  pl.BlockSpec((tm, LANE), lambda i: (i % pos_tiles, 0)),
                  pl.BlockSpec((1, GM_WIDTH), const2),
                  pl.BlockSpec((GM_GROUPS, GM_CHUNK, GM_CHUNK), const3),
                  pl.BlockSpec((GM_GROUPS, GM_CHUNK, GM_CH), const3),
                  pl.BlockSpec((1, MLA_Q_RANK), const2),
                  pl.BlockSpec((MLA_Q_RANK, MLA_HEADS * MLA_QH), const2),
                  pl.BlockSpec((1, MLA_KV_RANK), const2),
                  pl.BlockSpec((MLA_KV_RANK, MLA_HEADS * (MLA_NOPE + MLA_V)), const2)],
        out_specs=[pl.BlockSpec((tm, w_), row) for w_ in widths],
        out_shape=[jax.ShapeDtypeStruct((m, w_), BF16) for w_ in widths],
        scratch_shapes=[pltpu.VMEM((tm, D_MODEL), BF16)],
        compiler_params=_cparams(("parallel",)),
        name="mix_in",
    )(x, g, mod, mod, w, wkr, cos_t, sin_t, ggm, wsp, bsp, gq, wuq, gkv, wukv)


def _kv_specs(arrays, width):
    return [pl.BlockSpec((a.shape[0] // BATCH, width), lambda b, h, t: (b, h)) for a in arrays]


def _da_attn_call(lamv, g, q, ks, vs, lam_init):
    nq = q.shape[0] // BATCH
    tq = min(ATTN_TQ, nq)
    nqt = nq // tq
    qspec = pl.BlockSpec((tq, LANE), lambda b, h, t: (b * nqt + t, h))
    return pl.pallas_call(
        functools.partial(_da_attn_kernel, n_pieces=len(ks), lam_init=lam_init, sub=min(ATTN_SUB, tq)),
        grid=(BATCH, DA_HEADS, nqt),
        in_specs=[pl.BlockSpec(lamv.shape, lambda b, h, t: (0, 0)),
                  pl.BlockSpec((1, DA_V_DIM), lambda b, h, t: (0, 0)),
                  qspec] + _kv_specs(ks, LANE) + _kv_specs(vs, LANE),
        out_specs=qspec,
        out_shape=jax.ShapeDtypeStruct((q.shape[0], DA_WIDTH), BF16),
        compiler_params=_cparams(("parallel", "parallel", "arbitrary")),
        name="da_attn",
    )(lamv, g, q, *ks, *vs)


def _mla_attn_call(qm, ks, vms):
    nq = qm.shape[0] // BATCH
    tq = min(MLA_TQ, nq)
    nqt = nq // tq
    return pl.pallas_call(
        functools.partial(_mla_attn_kernel, n_pieces=len(ks), sub=min(MLA_SUB, tq)),
        grid=(BATCH, MLA_HEADS, nqt),
        in_specs=[pl.BlockSpec((tq, MLA_QH), lambda b, h, t: (b * nqt + t, h))]
        + _kv_specs(ks, MLA_QH) + _kv_specs(vms, LANE),
        out_specs=pl.BlockSpec((tq, MLA_V), lambda b, h, t: (b * nqt + t, h)),
        out_shape=jax.ShapeDtypeStruct((qm.shape[0], MLA_WIDTH), BF16),
        compiler_params=_cparams(("parallel", "parallel", "arbitrary")),
        name="mla_attn",
    )(qm, *ks, *vms)


def _outproj_call(x, mod, mod_rows, o_da, gated, o_mla, w, layer):
    m = x.shape[0]
    tm = 256
    row = lambda i: (i, 0)
    return pl.pallas_call(
        _outproj_kernel,
        grid=(m // tm,),
        in_specs=[pl.BlockSpec((tm, D_MODEL), row), _mod_spec(mod_rows, tm, 2),
                  pl.BlockSpec((tm, DA_WIDTH), row), pl.BlockSpec((tm, GM_WIDTH), row),
                  pl.BlockSpec((tm, MLA_WIDTH), row),
                  pl.BlockSpec((None, MIX_WIDTH, D_MODEL), lambda i: (layer, 0, 0),
                               pipeline_mode=pl.Buffered(1))],
        out_specs=pl.BlockSpec((tm, D_MODEL), row),
        out_shape=jax.ShapeDtypeStruct((m, D_MODEL), F32),
        scratch_shapes=[pltpu.VMEM((MIX_WIDTH, D_MODEL), BF16)],
        compiler_params=_cparams(("arbitrary",)),
        name="outproj",
    )(x, mod, o_da, gated, o_mla, w)


def _mlp_call(x, g, mod, mod_rows, g_final, w1, w2, layer, final_norm):
    m = x.shape[0]
    tm, tf = 1024, 512
    row = lambda i, f: (i, 0)
    return pl.pallas_call(
        functools.partial(_mlp_kernel, final_norm=final_norm),
        grid=(m // tm, D_FF // tf),
        in_specs=[pl.BlockSpec((tm, D_MODEL), row, pipeline_mode=pl.Buffered(1)),
                  pl.BlockSpec((1, D_MODEL), lambda i, f: (0, 0)),
                  _mod_spec(mod_rows, tm, 3), _mod_spec(mod_rows, tm, 4), _mod_spec(mod_rows, tm, 5),
                  pl.BlockSpec((1, D_MODEL), lambda i, f: (0, 0)),
                  pl.BlockSpec((None, D_MODEL, tf), lambda i, f: (layer, 0, f)),
                  pl.BlockSpec((None, tf, D_MODEL), lambda i, f: (layer, f, 0))],
        out_specs=pl.BlockSpec((tm, D_MODEL), row),
        out_shape=jax.ShapeDtypeStruct((m, D_MODEL), F32),
        scratch_shapes=[pltpu.VMEM((tm, D_MODEL), BF16)],
        compiler_params=_cparams(("parallel", "arbitrary")),
        name="mlp",
    )(x, g, mod, mod, mod, g_final, w1, w2)


def kernel(x, c, ctx, c_ctx, w_mod, b_mod, g_norm_mix, g_norm_mlp, w_in, lam_q1, lam_k1, lam_q2, lam_k2,
           g_da_sub, g_gm_v, w_spatial, b_spatial, g_mla_q, w_mla_uq, g_mla_kv, w_mla_ukv, w_out, w_fc1,
           w_fc2, g_final):
    cos_t, sin_t = _rope_tables()
    cos_c = jnp.ones((CTX_LEN, LANE), F32)
    sin_c = jnp.zeros((CTX_LEN, LANE), F32)

    xl = x.reshape(BATCH * SEQ, D_MODEL)
    xc = ctx.reshape(BATCH * CTX_LEN, D_MODEL)
    cvec = jnp.zeros((MOD_ROWS, D_MODEL), F32).at[:BATCH].set(c).at[CTX_MOD_ROW].set(c_ctx)
    mod_all = _mod_call(cvec, w_mod, b_mod)

    g_final2 = g_final.reshape(1, D_MODEL)
    w_in_p = w_in.astype(BF16)
    w_kr_p = _w_kr_layout(w_in)

    for l in range(DEPTH):
        last = l == DEPTH - 1
        lam_init = 0.8 - 0.6 * math.exp(-0.3 * l)
        mod = mod_all[l].reshape(MOD_ROWS, 1, 6 * D_MODEL)
        wuq_p = _w_uq_layout(w_mla_uq[l])
        wukv_p = w_mla_ukv[l].astype(BF16)
        g_mix, g_mlp = g_norm_mix[l].reshape(1, D_MODEL), g_norm_mlp[l].reshape(1, D_MODEL)
        lamv = jnp.stack([lam_q1[l], lam_k1[l], lam_q2[l], lam_k2[l]])
        g_sub = g_da_sub[l].reshape(1, DA_V_DIM)
        post_w = (g_gm_v[l].reshape(1, GM_WIDTH), w_spatial[l].astype(BF16),
                  jnp.broadcast_to(b_spatial[l][:, :, None], (GM_GROUPS, GM_CHUNK, GM_CH)),
                  g_mla_q[l].reshape(1, MLA_Q_RANK), wuq_p, g_mla_kv[l].reshape(1, MLA_KV_RANK), wukv_p)

        q, k, v, gated, qm, kn, vm = _mix_in_call(xl, g_mix, mod, LATENT_ROWS, w_in_p, w_kr_p, l, cos_t, sin_t,
                                                  *post_w)
        qc, kc, vc, gated_c, qmc, knc, vmc = _mix_in_call(xc, g_mix, mod, CONTEXT_ROWS, w_in_p, w_kr_p, l, cos_c,
                                                          sin_c, *post_w)

        o_da = _da_attn_call(lamv, g_sub, q, [k, kc], [v, vc], lam_init)
        o_mla = _mla_attn_call(qm, [kn, knc], [vm, vmc])
        xl = _outproj_call(xl, mod, LATENT_ROWS, o_da, gated, o_mla, w_out, l)
        xl = _mlp_call(xl, g_mlp, mod, LATENT_ROWS, g_final2, w_fc1, w_fc2, l, final_norm=last)

        if not last:
            o_da_c = _da_attn_call(lamv, g_sub, qc, [kc], [vc], lam_init)
            o_mla_c = _mla_attn_call(qmc, [knc], [vmc])
            xc = _outproj_call(xc, mod, CONTEXT_ROWS, o_da_c, gated_c, o_mla_c, w_out, l)
            xc = _mlp_call(xc, g_mlp, mod, CONTEXT_ROWS, g_final2, w_fc1, w_fc2, l, final_norm=False)

    return xl.reshape(BATCH, SEQ, D_MODEL)
```
